```python
import jax, jax.numpy as jnp
from jax import lax
import numpy as np

D_MODEL = 1024
BATCH = 4
SEQ = 8192
DEPTH = 2

D_FF = 2816
D_MIX = D_MODEL
POOL_CH = D_MIX // 2
POOL_WINDOWS = (2, 4, 8, 16)
POOL_GROUPS = len(POOL_WINDOWS)
POOL_GC = POOL_CH // POOL_GROUPS
CONV_CH = D_MIX - POOL_CH
CONV_WIDTH = 31
AB_IN = POOL_CH + 2 * CONV_CH
SGU_CH = D_MIX
SGU_HEADS = 8
SGU_HC = SGU_CH // SGU_HEADS
CHUNK = 128
N_EVEN = (DEPTH + 1) // 2
N_ODD = DEPTH // 2
EPS = 1e-6

kernel_name = "hybrid_pool_conv_sgu_macaron"


def rms_norm(x, g):
    xf = x.astype(jnp.float32)
    y = xf * lax.rsqrt(jnp.mean(xf * xf, axis=-1, keepdims=True) + EPS)
    return (y * g.astype(jnp.float32)).astype(x.dtype)


def layer_norm(x, g, b):
    xf = x.astype(jnp.float32)
    mu = jnp.mean(xf, axis=-1, keepdims=True)
    var = jnp.mean(jnp.square(xf - mu), axis=-1, keepdims=True)
    y = (xf - mu) * lax.rsqrt(var + EPS)
    return (y * g.astype(jnp.float32) + b.astype(jnp.float32)).astype(x.dtype)


def swiglu_ffn(x, w_in, w_out):
    gate, up = jnp.split(x @ w_in, 2, axis=-1)
    return (jax.nn.silu(gate) * up) @ w_out


def pool_mixer(u, w, b, scale):
    bsz, t_len, _ = u.shape
    csum = jnp.cumsum(u.astype(jnp.float32), axis=1)
    pos = jnp.arange(t_len, dtype=jnp.int32)
    means = []
    for g, win in enumerate(POOL_WINDOWS):
        cg = csum[..., g * POOL_GC:(g + 1) * POOL_GC]
        shifted = jnp.pad(cg[:, :t_len - win], ((0, 0), (win, 0), (0, 0)))
        count = jnp.minimum(pos + 1, win).astype(jnp.float32)[None, :, None]
        means.append((cg - shifted) / count)
    pooled = jnp.concatenate(means, axis=-1).astype(u.dtype) - u
    pooled = pooled.reshape(bsz, t_len, POOL_GROUPS, POOL_GC)
    mixed = jnp.einsum('btgc,gcd->btgd', pooled, w) + b
    return mixed.reshape(bsz, t_len, POOL_CH) * scale


def conv_module(h, conv_w, conv_b, ln_g, ln_b):
    a, gate = jnp.split(h, 2, axis=-1)
    g = a * jax.nn.sigmoid(gate)
    y = lax.conv_general_dilated(
        g, conv_w[:, None, :], window_strides=(1,), padding=((CONV_WIDTH - 1, 0),),
        dimension_numbers=('NWC', 'WIO', 'NWC'), feature_group_count=CONV_CH) + conv_b
    return jax.nn.silu(layer_norm(y, ln_g, ln_b))


def pool_conv_mixer(xn, w_in, pool_w, pool_b, pool_scale, conv_w, conv_b, ln_g, ln_b, w_out):
    h = xn @ w_in
    ya = pool_mixer(h[..., :POOL_CH], pool_w, pool_b, pool_scale)
    yb = conv_module(h[..., POOL_CH:], conv_w, conv_b, ln_g, ln_b)
    return jnp.concatenate([ya, yb], axis=-1) @ w_out


def sgu_mixer(xn, w_in, ln_g, ln_b, w_s, b_s, w_out):
    bsz, t_len, _ = xn.shape
    z = jax.nn.gelu(xn @ w_in, approximate=False)
    u, v = jnp.split(z, 2, axis=-1)
    v = layer_norm(v, ln_g, ln_b)
    v = v.reshape(bsz, t_len // CHUNK, CHUNK, SGU_HEADS, SGU_HC)
    mask = jnp.tril(jnp.ones((CHUNK, CHUNK), dtype=w_s.dtype))
    w_masked = w_s * mask
    v = jnp.einsum('hst,bnthc->bnshc', w_masked, v) + b_s.T[None, None, :, :, None]
    v = v.reshape(bsz, t_len, SGU_CH)
    return (u * v) @ w_out


def setup_inputs(seed: int = 0) -> dict:
    key = jax.random.key(seed)
    ks = iter(jax.random.split(key, 32))

    def nrm(shape, scale):
        return jax.random.normal(next(ks), shape, jnp.float32) * scale

    def gain(shape):
        return 1.0 + nrm(shape, 0.05)

    return {
        "x": nrm((BATCH, SEQ, D_MODEL), 1.0),
        "ffn1_norm": gain((DEPTH, D_MODEL)),
        "ffn1_w_in": nrm((DEPTH, D_MODEL, 2 * D_FF), D_MODEL ** -0.5),
        "ffn1_w_out": nrm((DEPTH, D_FF, D_MODEL), D_FF ** -0.5),
        "mix_norm": gain((DEPTH, D_MODEL)),
        "ffn2_norm": gain((DEPTH, D_MODEL)),
        "ffn2_w_in": nrm((DEPTH, D_MODEL, 2 * D_FF), D_MODEL ** -0.5),
        "ffn2_w_out": nrm((DEPTH, D_FF, D_MODEL), D_FF ** -0.5),
        "ab_w_in": nrm((N_EVEN, D_MODEL, AB_IN), D_MODEL ** -0.5),
        "pool_w": nrm((N_EVEN, POOL_GROUPS, POOL_GC, POOL_GC), POOL_GC ** -0.5),
        "pool_b": nrm((N_EVEN, POOL_GROUPS, POOL_GC), 0.02),
        "pool_scale": 0.5 + nrm((N_EVEN, POOL_CH), 0.05),
        "conv_w": nrm((N_EVEN, CONV_WIDTH, CONV_CH), CONV_WIDTH ** -0.5),
        "conv_b": nrm((N_EVEN, CONV_CH), 0.02),
        "conv_ln_g": gain((N_EVEN, CONV_CH)),
        "conv_ln_b": nrm((N_EVEN, CONV_CH), 0.02),
        "ab_w_out": nrm((N_EVEN, D_MIX, D_MODEL), D_MIX ** -0.5),
        "sgu_w_in": nrm((N_ODD, D_MODEL, 2 * SGU_CH), D_MODEL ** -0.5),
        "sgu_ln_g": gain((N_ODD, SGU_CH)),
        "sgu_ln_b": nrm((N_ODD, SGU_CH), 0.02),
        "sgu_w": nrm((N_ODD, SGU_HEADS, CHUNK, CHUNK), CHUNK ** -0.5),
        "sgu_b": 1.0 + nrm((N_ODD, SGU_HEADS, CHUNK), 0.05),
        "sgu_w_out": nrm((N_ODD, SGU_CH, D_MODEL), SGU_CH ** -0.5),
        "final_norm": gain((D_MODEL,)),
    }


def reference(x, ffn1_norm, ffn1_w_in, ffn1_w_out, mix_norm, ffn2_norm, ffn2_w_in, ffn2_w_out,
              ab_w_in, pool_w, pool_b, pool_scale, conv_w, conv_b, conv_ln_g, conv_ln_b, ab_w_out,
              sgu_w_in, sgu_ln_g, sgu_ln_b, sgu_w, sgu_b, sgu_w_out, final_norm):
    for i in range(DEPTH):
        x = x + 0.5 * swiglu_ffn(rms_norm(x, ffn1_norm[i]), ffn1_w_in[i], ffn1_w_out[i])
        xn = rms_norm(x, mix_norm[i])
        if i % 2 == 0:
            j = i // 2
            y = pool_conv_mixer(xn, ab_w_in[j], pool_w[j], pool_b[j], pool_scale[j],
                                conv_w[j], conv_b[j], conv_ln_g[j], conv_ln_b[j], ab_w_out[j])
        else:
            j = i // 2
            y = sgu_mixer(xn, sgu_w_in[j], sgu_ln_g[j], sgu_ln_b[j], sgu_w[j], sgu_b[j], sgu_w_out[j])
        x = x + y
        x = x + 0.5 * swiglu_ffn(rms_norm(x, ffn2_norm[i]), ffn2_w_in[i], ffn2_w_out[i])
    return rms_norm(x, final_norm)
```

```python
import functools

import jax
import jax.numpy as jnp
from jax import lax
from jax.experimental import pallas as pl
from jax.experimental.pallas import tpu as pltpu

D_MODEL = 1024
D_FF = 2816
POOL_CH = 512
POOL_WINDOWS = (2, 4, 8, 16)
POOL_GC = 128
CONV_CH = 512
CONV_WIDTH = 31
AB_IN = POOL_CH + 2 * CONV_CH
SGU_CH = 1024
SGU_HEADS = 8
SGU_HC = 128
CHUNK = 128
EPS = 1e-6

MXU_TILE = 256
FF_CHUNK = MXU_TILE
N_FF_CHUNKS = D_FF // FF_CHUNK

TM_FFN = 512
TM_MIX = 512
TM_SGU = 512

HALO = 32
CONV_RB = 64

VMEM_LIMIT_BYTES = 56 * 1024 * 1024

F32 = jnp.float32
BF16 = jnp.bfloat16


def _rms(x, g):
    return x * lax.rsqrt(jnp.mean(x * x, axis=-1, keepdims=True) + EPS) * g


def _layer_norm(x, g, b):
    mu = jnp.mean(x, axis=-1, keepdims=True)
    xc = x - mu
    var = jnp.mean(xc * xc, axis=-1, keepdims=True)
    return xc * lax.rsqrt(var + EPS) * g + b


def _gelu(x):
    return 0.5 * x * (1.0 + lax.erf(x * (2.0 ** -0.5)))


def _const_spec(shape):
    nd = len(shape)
    return pl.BlockSpec(shape, lambda *_: (0,) * nd, pipeline_mode=pl.Buffered(1))


def _ffn_body(x_ref, g_ref, win_ref, wout_ref, fin_ref, o_ref, a_ref):
    x = x_ref[...]
    xn = _rms(x, g_ref[...]).astype(BF16)
    for c in range(N_FF_CHUNKS):
        h = jnp.dot(xn, win_ref[c], preferred_element_type=F32)
        gate = h[:, :FF_CHUNK]
        up = h[:, FF_CHUNK:]
        a_ref[:, c * FF_CHUNK:(c + 1) * FF_CHUNK] = (jax.nn.silu(gate) * up).astype(BF16)
    y = jnp.dot(a_ref[...], wout_ref[...], preferred_element_type=F32)
    out = x + 0.5 * y
    if fin_ref is not None:
        out = _rms(out, fin_ref[...])
    o_ref[...] = out


def _ffn_kernel(x_ref, g_ref, win_ref, wout_ref, o_ref, a_ref):
    _ffn_body(x_ref, g_ref, win_ref, wout_ref, None, o_ref, a_ref)


def _ffn_final_kernel(x_ref, g_ref, win_ref, wout_ref, fin_ref, o_ref, a_ref):
    _ffn_body(x_ref, g_ref, win_ref, wout_ref, fin_ref, o_ref, a_ref)


def _ffn(x2, gain, w_in, w_out, final_gain=None):
    n_tok = x2.shape[0]
    wg = w_in[:, :D_FF].reshape(D_MODEL, N_FF_CHUNKS, FF_CHUNK)
    wu = w_in[:, D_FF:].reshape(D_MODEL, N_FF_CHUNKS, FF_CHUNK)
    win = jnp.concatenate([wg, wu], axis=-1).transpose(1, 0, 2).astype(BF16)
    wout = w_out.astype(BF16)
    tile = pl.BlockSpec((TM_FFN, D_MODEL), lambda i: (i, 0))
    in_specs = [tile, _const_spec((1, D_MODEL)), _const_spec(win.shape), _const_spec(wout.shape)]
    args = [x2, gain.reshape(1, D_MODEL), win, wout]
    kern = _ffn_kernel
    if final_gain is not None:
        in_specs.append(_const_spec((1, D_MODEL)))
        args.append(final_gain.reshape(1, D_MODEL))
        kern = _ffn_final_kernel
    return pl.pallas_call(
        kern,
        grid=(n_tok // TM_FFN,),
        in_specs=in_specs,
        out_specs=tile,
        out_shape=jax.ShapeDtypeStruct(x2.shape, F32),
        scratch_shapes=[pltpu.VMEM((TM_FFN, D_FF), BF16)],
        compiler_params=pltpu.CompilerParams(
            dimension_semantics=("parallel",), vmem_limit_bytes=VMEM_LIMIT_BYTES),
        name="ffn_final" if final_gain is not None else "ffn",
    )(*args)


def _pool_conv_kernel(x_ref, g_ref, win_ref, pw_ref, pb_ref, ps_ref, cw_ref, cb_ref,
                      lg_ref, lb_ref, wout_ref, o_ref, ubuf, gbuf, ybuf):
    t = pl.program_id(1)

    @pl.when(t == 0)
    def _():
        ubuf[0:HALO, :] = jnp.zeros((HALO, POOL_CH), F32)
        gbuf[0:HALO, :] = jnp.zeros((HALO, CONV_CH), F32)

    x = x_ref[0]
    xn = _rms(x, g_ref[...]).astype(BF16)
    h = jnp.dot(xn, win_ref[...], preferred_element_type=F32)
    u = h[:, :POOL_CH]
    ubuf[HALO:, :] = u
    gbuf[HALO:, :] = h[:, POOL_CH:POOL_CH + CONV_CH] * jax.nn.sigmoid(h[:, POOL_CH + CONV_CH:])

    pos = t * TM_MIX + lax.broadcasted_iota(jnp.int32, (TM_MIX, 1), 0)
    pooled = []
    for gi, win in enumerate(POOL_WINDOWS):
        lanes = slice(gi * POOL_GC, (gi + 1) * POOL_GC)
        s = ubuf[pl.ds(HALO, TM_MIX), lanes]
        for j in range(1, win):
            s = s + ubuf[pl.ds(HALO - j, TM_MIX), lanes]
        count = jnp.minimum(pos + 1, win).astype(F32)
        pooled.append(s / count - u[:, lanes])
    ya = []
    for p in range(2):
        pp = jnp.concatenate(pooled[2 * p:2 * p + 2], axis=-1).astype(BF16)
        ya.append(jnp.dot(pp, pw_ref[p], preferred_element_type=F32))
    ya = (jnp.concatenate(ya, axis=-1) + pb_ref[...]) * ps_ref[...]

    for rb in range(TM_MIX // CONV_RB):
        for lb in range(CONV_CH // 128):
            lanes = slice(lb * 128, (lb + 1) * 128)
            acc = jnp.zeros((CONV_RB, 128), F32)
            for k in range(CONV_WIDTH):
                start = HALO - (CONV_WIDTH - 1) + k + rb * CONV_RB
                acc = acc + gbuf[pl.ds(start, CONV_RB), lanes] * cw_ref[k:k + 1, lanes]
            ybuf[rb * CONV_RB:(rb + 1) * CONV_RB, lanes] = acc
    y = ybuf[...] + cb_ref[...]
    yb = jax.nn.silu(_layer_norm(y, lg_ref[...], lb_ref[...]))

    cat = jnp.concatenate([ya.astype(BF16), yb.astype(BF16)], axis=-1)
    o_ref[0] = x + jnp.dot(cat, wout_ref[...], preferred_element_type=F32)

    ubuf[0:HALO, :] = ubuf[TM_MIX:TM_MIX + HALO, :]
    gbuf[0:HALO, :] = gbuf[TM_MIX:TM_MIX + HALO, :]


def _pool_conv_mixer(x, gain, w_in, pool_w, pool_b, pool_scale, conv_w, conv_b, ln_g, ln_b, w_out):
    bsz, t_len, _ = x.shape
    pw = pool_w.astype(BF16)
    z = jnp.zeros((POOL_GC, POOL_GC), BF16)
    pw_bd = jnp.stack([
        jnp.block([[pw[0], z], [z, pw[1]]]),
        jnp.block([[pw[2], z], [z, pw[3]]]),
    ])
    tile = pl.BlockSpec((1, TM_MIX, D_MODEL), lambda b, t: (b, t, 0))
    row = lambda n: _const_spec((1, n))
    return pl.pallas_call(
        _pool_conv_kernel,
        grid=(bsz, t_len // TM_MIX),
        in_specs=[tile, row(D_MODEL), _const_spec((D_MODEL, AB_IN)), _const_spec(pw_bd.shape),
                  row(POOL_CH), row(POOL_CH), _const_spec((CONV_WIDTH, CONV_CH)), row(CONV_CH),
                  row(CONV_CH), row(CONV_CH), _const_spec((D_MODEL, D_MODEL))],
        out_specs=tile,
        out_shape=jax.ShapeDtypeStruct(x.shape, F32),
        scratch_shapes=[pltpu.VMEM((HALO + TM_MIX, POOL_CH), F32),
                        pltpu.VMEM((HALO + TM_MIX, CONV_CH), F32),
                        pltpu.VMEM((TM_MIX, CONV_CH), F32)],
        compiler_params=pltpu.CompilerParams(
            dimension_semantics=("arbitrary", "arbitrary"), vmem_limit_bytes=VMEM_LIMIT_BYTES),
        name="pool_conv_mixer",
    )(x, gain.reshape(1, D_MODEL), w_in.astype(BF16), pw_bd, pool_b.reshape(1, POOL_CH),
      pool_scale.reshape(1, POOL_CH), conv_w, conv_b.reshape(1, CONV_CH),
      ln_g.reshape(1, CONV_CH), ln_b.reshape(1, CONV_CH), w_out.astype(BF16))


def _sgu_kernel(x_ref, g_ref, win_ref, lg_ref, lb_ref, ws_ref, bs_ref, wout_ref, o_ref, vo_ref):
    n_chunks = TM_SGU // CHUNK
    x = x_ref[...]
    xn = _rms(x, g_ref[...]).astype(BF16)
    z = _gelu(jnp.dot(xn, win_ref[...], preferred_element_type=F32))
    u = z[:, :SGU_CH]
    v = _layer_norm(z[:, SGU_CH:], lg_ref[...], lb_ref[...]).astype(BF16)

    row = lax.broadcasted_iota(jnp.int32, (CHUNK, CHUNK), 0)
    col = lax.broadcasted_iota(jnp.int32, (CHUNK, CHUNK), 1)
    mask = (col <= row).astype(F32)
    for hd in range(SGU_HEADS):
        lanes = slice(hd * SGU_HC, (hd + 1) * SGU_HC)
        w = (ws_ref[hd] * mask).astype(BF16)
        rhs = jnp.concatenate([v[n * CHUNK:(n + 1) * CHUNK, lanes] for n in range(n_chunks)], axis=-1)
        r = jnp.dot(w, rhs, preferred_element_type=F32)
        for n in range(n_chunks):
            vo_ref[n * CHUNK:(n + 1) * CHUNK, lanes] = r[:, n * SGU_HC:(n + 1) * SGU_HC]
    vo = vo_ref[...].reshape(n_chunks, CHUNK, SGU_CH) + bs_ref[...][None]
    gated = (u * vo.reshape(TM_SGU, SGU_CH)).astype(BF16)
    o_ref[...] = x + jnp.dot(gated, wout_ref[...], preferred_element_type=F32)


def _sgu_mixer(x2, gain, w_in, ln_g, ln_b, w_s, b_s, w_out):
    bias = jnp.repeat(b_s.T, SGU_HC, axis=1)
    tile = pl.BlockSpec((TM_SGU, D_MODEL), lambda i: (i, 0))
    return pl.pallas_call(
        _sgu_kernel,
        grid=(x2.shape[0] // TM_SGU,),
        in_specs=[tile, _const_spec((1, D_MODEL)), _const_spec((D_MODEL, 2 * SGU_CH)),
                  _const_spec((1, SGU_CH)), _const_spec((1, SGU_CH)),
                  _const_spec((SGU_HEADS, CHUNK, CHUNK)), _const_spec((CHUNK, SGU_CH)),
                  _const_spec((SGU_CH, D_MODEL))],
        out_specs=tile,
        out_shape=jax.ShapeDtypeStruct(x2.shape, F32),
        scratch_shapes=[pltpu.VMEM((TM_SGU, SGU_CH), F32)],
        compiler_params=pltpu.CompilerParams(
            dimension_semantics=("parallel",), vmem_limit_bytes=VMEM_LIMIT_BYTES),
        name="sgu_mixer",
    )(x2, gain.reshape(1, D_MODEL), w_in.astype(BF16), ln_g.reshape(1, SGU_CH),
      ln_b.reshape(1, SGU_CH), w_s, bias, w_out.astype(BF16))


def kernel(x, ffn1_norm, ffn1_w_in, ffn1_w_out, mix_norm, ffn2_norm, ffn2_w_in, ffn2_w_out,
           ab_w_in, pool_w, pool_b, pool_scale, conv_w, conv_b, conv_ln_g, conv_ln_b, ab_w_out,
           sgu_w_in, sgu_ln_g, sgu_ln_b, sgu_w, sgu_b, sgu_w_out, final_norm):
    bsz, t_len, d = x.shape
    depth = ffn1_norm.shape[0]
    assert d == D_MODEL and t_len % max(TM_MIX, TM_SGU, TM_FFN) == 0
    n_tok = bsz * t_len
    for i in range(depth):
        j = i // 2
        x = _ffn(x.reshape(n_tok, d), ffn1_norm[i], ffn1_w_in[i], ffn1_w_out[i])
        if i % 2 == 0:
            x = _pool_conv_mixer(x.reshape(bsz, t_len, d), mix_norm[i], ab_w_in[j], pool_w[j],
                                 pool_b[j], pool_scale[j], conv_w[j], conv_b[j], conv_ln_g[j],
                                 conv_ln_b[j], ab_w_out[j])
        else:
            x = _sgu_mixer(x.reshape(n_tok, d), mix_norm[i], sgu_w_in[j], sgu_ln_g[j],
                           sgu_ln_b[j], sgu_w[j], sgu_b[j], sgu_w_out[j])
        final = final_norm if i == depth - 1 else None
        x = _ffn(x.reshape(n_tok, d), ffn2_norm[i], ffn2_w_in[i], ffn2_w_out[i], final)
    return x.reshape(bsz, t_len, d)
```

```python
import jax
import jax.numpy as jnp
from jax import lax
from jax.experimental import pallas as pl
from jax.experimental.pallas import tpu as pltpu

D_MODEL = 1024
D_FF = 2816
POOL_CH = 512
POOL_WINDOWS = (2, 4, 8, 16)
POOL_GC = 128
CONV_CH = 512
CONV_WIDTH = 31
AB_IN = POOL_CH + 2 * CONV_CH
SGU_CH = 1024
SGU_HEADS = 8
SGU_HC = 128
CHUNK = 128
EPS = 1e-6

MXU_TILE = 256
FF_CHUNK = MXU_TILE
N_FF_CHUNKS = D_FF // FF_CHUNK

TM_FFN = 512
TM_MIX = 512
TM_SGU = 512

HALO = 32
CONV_RB = 64
CONV_SH_ROWS = TM_MIX + 24

VMEM_LIMIT_BYTES = 56 * 1024 * 1024

F32 = jnp.float32
BF16 = jnp.bfloat16


def _rms(x, g):
    return x * lax.rsqrt(jnp.mean(x * x, axis=-1, keepdims=True) + EPS) * g


def _layer_norm(x, g, b):
    mu = jnp.mean(x, axis=-1, keepdims=True)
    xc = x - mu
    var = jnp.mean(xc * xc, axis=-1, keepdims=True)
    return xc * lax.rsqrt(var + EPS) * g + b


def _gelu(x):
    return 0.5 * x * (1.0 + lax.erf(x * (2.0 ** -0.5)))


def _const_spec(shape):
    nd = len(shape)
    return pl.BlockSpec(shape, lambda *_: (0,) * nd, pipeline_mode=pl.Buffered(1))


def _ffn_body(x_ref, g_ref, win_ref, wout_ref, fin_ref, o_ref, a_ref):
    x = x_ref[...]
    xn = _rms(x, g_ref[...]).astype(BF16)
    for c in range(N_FF_CHUNKS):
        cols = slice(c * FF_CHUNK, (c + 1) * FF_CHUNK)
        up_cols = slice(D_FF + c * FF_CHUNK, D_FF + (c + 1) * FF_CHUNK)
        gate = jnp.dot(xn, win_ref[:, cols], preferred_element_type=F32)
        up = jnp.dot(xn, win_ref[:, up_cols], preferred_element_type=F32)
        a_ref[:, cols] = (jax.nn.silu(gate) * up).astype(BF16)
    y = jnp.dot(a_ref[...], wout_ref[...], preferred_element_type=F32)
    out = x + 0.5 * y
    if fin_ref is not None:
        out = _rms(out, fin_ref[...])
    o_ref[...] = out


def _ffn_kernel(x_ref, g_ref, win_ref, wout_ref, o_ref, a_ref):
    _ffn_body(x_ref, g_ref, win_ref, wout_ref, None, o_ref, a_ref)


def _ffn_final_kernel(x_ref, g_ref, win_ref, wout_ref, fin_ref, o_ref, a_ref):
    _ffn_body(x_ref, g_ref, win_ref, wout_ref, fin_ref, o_ref, a_ref)


def _ffn(x2, gain, w_in, w_out, final_gain=None):
    n_tok = x2.shape[0]
    win = w_in.astype(BF16)
    wout = w_out.astype(BF16)
    tile = pl.BlockSpec((TM_FFN, D_MODEL), lambda i: (i, 0))
    in_specs = [tile, _const_spec((1, D_MODEL)), _const_spec(win.shape), _const_spec(wout.shape)]
    args = [x2, gain.reshape(1, D_MODEL), win, wout]
    kern = _ffn_kernel
    if final_gain is not None:
        in_specs.append(_const_spec((1, D_MODEL)))
        args.append(final_gain.reshape(1, D_MODEL))
        kern = _ffn_final_kernel
    return pl.pallas_call(
        kern,
        grid=(n_tok // TM_FFN,),
        in_specs=in_specs,
        out_specs=tile,
        out_shape=jax.ShapeDtypeStruct(x2.shape, F32),
        scratch_shapes=[pltpu.VMEM((TM_FFN, D_FF), BF16)],
        compiler_params=pltpu.CompilerParams(
            dimension_semantics=("parallel",), vmem_limit_bytes=VMEM_LIMIT_BYTES),
        name="ffn_final" if final_gain is not None else "ffn",
    )(*args)


def _pool_conv_kernel(x_ref, g_ref, win_ref, pw_ref, pb_ref, ps_ref, cw_ref, cb_ref,
                      lg_ref, lb_ref, wout_ref, o_ref, ubuf, s2buf, s4buf, s8buf, gsh, ybuf):
    t = pl.program_id(1)
    rows = HALO + TM_MIX

    @pl.when(t == 0)
    def _():
        ubuf[0:HALO, :] = jnp.zeros((HALO, POOL_CH), F32)
        gsh[0, 0:HALO, :] = jnp.zeros((HALO, CONV_CH), F32)

    x = x_ref[0]
    xn = _rms(x, g_ref[...]).astype(BF16)
    h = jnp.dot(xn, win_ref[...], preferred_element_type=F32)
    u = h[:, :POOL_CH]
    ubuf[HALO:, :] = u
    gsh[0, HALO:, :] = h[:, POOL_CH:POOL_CH + CONV_CH] * jax.nn.sigmoid(h[:, POOL_CH + CONV_CH:])

    g1 = POOL_GC
    s2buf[8:, :] = ubuf[8:, :] + ubuf[pl.ds(7, rows - 8), :]
    s4buf[16:, :] = s2buf[16:, g1:] + s2buf[pl.ds(14, rows - 16), g1:]
    s8buf[24:, :] = s4buf[24:, g1:] + s4buf[pl.ds(20, rows - 24), g1:]
    s16 = s8buf[HALO:, g1:] + s8buf[pl.ds(HALO - 8, TM_MIX), g1:]
    sums = [s2buf[HALO:, :g1], s4buf[HALO:, :g1], s8buf[HALO:, :g1], s16]
    pos = t * TM_MIX + lax.broadcasted_iota(jnp.int32, (TM_MIX, 1), 0)
    pooled = []
    for gi, win in enumerate(POOL_WINDOWS):
        count = jnp.minimum(pos + 1, win).astype(F32)
        pooled.append(sums[gi] / count - u[:, gi * POOL_GC:(gi + 1) * POOL_GC])
    ya = []
    for p in range(2):
        pp = jnp.concatenate(pooled[2 * p:2 * p + 2], axis=-1).astype(BF16)
        ya.append(jnp.dot(pp, pw_ref[p], preferred_element_type=F32))
    ya = (jnp.concatenate(ya, axis=-1) + pb_ref[...]) * ps_ref[...]

    for r in range(1, 8):
        gsh[r, 0:CONV_SH_ROWS, :] = gsh[0, pl.ds(r, CONV_SH_ROWS), :]

    taps_by_shift = [[] for _ in range(8)]
    for k in range(CONV_WIDTH):
        q, r = divmod(HALO - (CONV_WIDTH - 1) + k, 8)
        taps_by_shift[r].append((k, q))
    def conv_rows(rb, carry):
        base = pl.multiple_of(rb * CONV_RB, CONV_RB)
        for lb in range(CONV_CH // 128):
            lanes = slice(lb * 128, (lb + 1) * 128)
            acc = None
            for r, taps in enumerate(taps_by_shift):
                span = CONV_RB + 8 * max(q for _, q in taps)
                slab = gsh[r, pl.ds(base, span), lanes]
                for k, q in taps:
                    term = slab[8 * q:8 * q + CONV_RB] * cw_ref[k:k + 1, lanes]
                    acc = term if acc is None else acc + term
            ybuf[pl.ds(base, CONV_RB), lanes] = acc
        return carry

    lax.fori_loop(0, TM_MIX // CONV_RB, conv_rows, 0)
    y = ybuf[...] + cb_ref[...]
    yb = jax.nn.silu(_layer_norm(y, lg_ref[...], lb_ref[...]))

    cat = jnp.concatenate([ya.astype(BF16), yb.astype(BF16)], axis=-1)
    o_ref[0] = x + jnp.dot(cat, wout_ref[...], preferred_element_type=F32)

    ubuf[0:HALO, :] = ubuf[TM_MIX:, :]
    gsh[0, 0:HALO, :] = gsh[0, TM_MIX:, :]


def _pool_conv_mixer(x, gain, w_in, pool_w, pool_b, pool_scale, conv_w, conv_b, ln_g, ln_b, w_out):
    bsz, t_len, _ = x.shape
    rows = HALO + TM_MIX
    pw = pool_w.astype(BF16)
    z = jnp.zeros((POOL_GC, POOL_GC), BF16)
    pw_bd = jnp.stack([
        jnp.block([[pw[0], z], [z, pw[1]]]),
        jnp.block([[pw[2], z], [z, pw[3]]]),
    ])
    tile = pl.BlockSpec((1, TM_MIX, D_MODEL), lambda b, t: (b, t, 0))
    row = lambda n: _const_spec((1, n))
    return pl.pallas_call(
        _pool_conv_kernel,
        grid=(bsz, t_len // TM_MIX),
        in_specs=[tile, row(D_MODEL), _const_spec((D_MODEL, AB_IN)), _const_spec(pw_bd.shape),
                  row(POOL_CH), row(POOL_CH), _const_spec((CONV_WIDTH, CONV_CH)), row(CONV_CH),
                  row(CONV_CH), row(CONV_CH), _const_spec((D_MODEL, D_MODEL))],
        out_specs=tile,
        out_shape=jax.ShapeDtypeStruct(x.shape, F32),
        scratch_shapes=[pltpu.VMEM((rows, POOL_CH), F32),
                        pltpu.VMEM((rows, POOL_CH), F32),
                        pltpu.VMEM((rows, POOL_CH - POOL_GC), F32),
                        pltpu.VMEM((rows, POOL_CH - 2 * POOL_GC), F32),
                        pltpu.VMEM((8, rows, CONV_CH), F32),
                        pltpu.VMEM((TM_MIX, CONV_CH), F32)],
        compiler_params=pltpu.CompilerParams(
            dimension_semantics=("arbitrary", "arbitrary"), vmem_limit_bytes=VMEM_LIMIT_BYTES),
        name="pool_conv_mixer",
    )(x, gain.reshape(1, D_MODEL), w_in.astype(BF16), pw_bd, pool_b.reshape(1, POOL_CH),
      pool_scale.reshape(1, POOL_CH), conv_w, conv_b.reshape(1, CONV_CH),
      ln_g.reshape(1, CONV_CH), ln_b.reshape(1, CONV_CH), w_out.astype(BF16))


def _sgu_kernel(x_ref, g_ref, win_ref, lg_ref, lb_ref, ws_ref, bs_ref, wout_ref, o_ref, vo_ref):
    n_chunks = TM_SGU // CHUNK
    x = x_ref[...]
    xn = _rms(x, g_ref[...]).astype(BF16)
    z = _gelu(jnp.dot(xn, win_ref[...], preferred_element_type=F32))
    u = z[:, :SGU_CH]
    v = _layer_norm(z[:, SGU_CH:], lg_ref[...], lb_ref[...]).astype(BF16)

    row = lax.broadcasted_iota(jnp.int32, (CHUNK, CHUNK), 0)
    col = lax.broadcasted_iota(jnp.int32, (CHUNK, CHUNK), 1)
    mask = (col <= row).astype(F32)
    for hd in range(SGU_HEADS):
        lanes = slice(hd * SGU_HC, (hd + 1) * SGU_HC)
        w = (ws_ref[hd] * mask).astype(BF16)
        rhs = jnp.concatenate([v[n * CHUNK:(n + 1) * CHUNK, lanes] for n in range(n_chunks)], axis=-1)
        r = jnp.dot(w, rhs, preferred_element_type=F32)
        for n in range(n_chunks):
            vo_ref[n * CHUNK:(n + 1) * CHUNK, lanes] = r[:, n * SGU_HC:(n + 1) * SGU_HC]
    vo = vo_ref[...].reshape(n_chunks, CHUNK, SGU_CH) + bs_ref[...][None]
    gated = (u * vo.reshape(TM_SGU, SGU_CH)).astype(BF16)
    o_ref[...] = x + jnp.dot(gated, wout_ref[...], preferred_element_type=F32)


def _sgu_mixer(x2, gain, w_in, ln_g, ln_b, w_s, b_s, w_out):
    bias = jnp.repeat(b_s.T, SGU_HC, axis=1)
    tile = pl.BlockSpec((TM_SGU, D_MODEL), lambda i: (i, 0))
    return pl.pallas_call(
        _sgu_kernel,
        grid=(x2.shape[0] // TM_SGU,),
        in_specs=[tile, _const_spec((1, D_MODEL)), _const_spec((D_MODEL, 2 * SGU_CH)),
                  _const_spec((1, SGU_CH)), _const_spec((1, SGU_CH)),
                  _const_spec((SGU_HEADS, CHUNK, CHUNK)), _const_spec((CHUNK, SGU_CH)),
                  _const_spec((SGU_CH, D_MODEL))],
        out_specs=tile,
        out_shape=jax.ShapeDtypeStruct(x2.shape, F32),
        scratch_shapes=[pltpu.VMEM((TM_SGU, SGU_CH), F32)],
        compiler_params=pltpu.CompilerParams(
            dimension_semantics=("parallel",), vmem_limit_bytes=VMEM_LIMIT_BYTES),
        name="sgu_mixer",
    )(x2, gain.reshape(1, D_MODEL), w_in.astype(BF16), ln_g.reshape(1, SGU_CH),
      ln_b.reshape(1, SGU_CH), w_s, bias, w_out.astype(BF16))


def kernel(x, ffn1_norm, ffn1_w_in, ffn1_w_out, mix_norm, ffn2_norm, ffn2_w_in, ffn2_w_out,
           ab_w_in, pool_w, pool_b, pool_scale, conv_w, conv_b, conv_ln_g, conv_ln_b, ab_w_out,
           sgu_w_in, sgu_ln_g, sgu_ln_b, sgu_w, sgu_b, sgu_w_out, final_norm):
    bsz, t_len, d = x.shape
    depth = ffn1_norm.shape[0]
    assert d == D_MODEL and t_len % max(TM_MIX, TM_SGU, TM_FFN) == 0
    n_tok = bsz * t_len
    for i in range(depth):
        j = i // 2
        x = _ffn(x.reshape(n_tok, d), ffn1_norm[i], ffn1_w_in[i], ffn1_w_out[i])
        if i % 2 == 0:
            x = _pool_conv_mixer(x.reshape(bsz, t_len, d), mix_norm[i], ab_w_in[j], pool_w[j],
                                 pool_b[j], pool_scale[j], conv_w[j], conv_b[j], conv_ln_g[j],
                                 conv_ln_b[j], ab_w_out[j])
        else:
            x = _sgu_mixer(x.reshape(n_tok, d), mix_norm[i], sgu_w_in[j], sgu_ln_g[j],
                           sgu_ln_b[j], sgu_w[j], sgu_b[j], sgu_w_out[j])
        final = final_norm if i == depth - 1 else None
        x = _ffn(x.reshape(n_tok, d), ffn2_norm[i], ffn2_w_in[i], ffn2_w_out[i], final)
    return x.reshape(bsz, t_len, d)
```

```python
import jax
import jax.numpy as jnp
from jax import lax
from jax.experimental import pallas as pl
from jax.experimental.pallas import tpu as pltpu

D_MODEL = 1024
D_FF = 2816
POOL_CH = 512
POOL_WINDOWS = (2, 4, 8, 16)
POOL_GC = 128
CONV_CH = 512
CONV_WIDTH = 31
AB_IN = POOL_CH + 2 * CONV_CH
SGU_CH = 1024
SGU_HEADS = 8
SGU_HC = 128
CHUNK = 128
EPS = 1e-6

MXU_TILE = 256
FF_CHUNK = MXU_TILE
N_FF_CHUNKS = D_FF // FF_CHUNK

TM_FFN = 512
TM_MIX = 512
TM_SGU = 512

HALO = 32
CONV_RB = 64
CONV_SH_ROWS = TM_MIX + 24

N_CAST_CHUNKS = 16
BF16_SUBLANES = 16

VMEM_LIMIT_BYTES = 56 * 1024 * 1024

F32 = jnp.float32
BF16 = jnp.bfloat16


def _rms(x, g):
    return x * lax.rsqrt(jnp.mean(x * x, axis=-1, keepdims=True) + EPS) * g


def _norm_parts(x, g):
    rinv = lax.rsqrt(jnp.mean(x * x, axis=-1, keepdims=True) + EPS)
    return (x * g).astype(BF16), rinv


def _layer_norm(x, g, b):
    mu = jnp.mean(x, axis=-1, keepdims=True)
    xc = x - mu
    var = jnp.mean(xc * xc, axis=-1, keepdims=True)
    return xc * lax.rsqrt(var + EPS) * g + b


def _gelu(x):
    return 0.5 * x * (1.0 + lax.erf(x * (2.0 ** -0.5)))


def _const_spec(shape):
    nd = len(shape)
    return pl.BlockSpec(shape, lambda *_: (0,) * nd, pipeline_mode=pl.Buffered(1))


def _tile_spec(tm):
    return pl.BlockSpec((tm, D_MODEL), lambda i: (i, 0))


def _cast_plan(w_stack, layer):
    _, r, c = w_stack.shape
    assert r % (N_CAST_CHUNKS * BF16_SUBLANES) == 0
    rows = r // N_CAST_CHUNKS
    chunk = lambda i: jnp.minimum(i, N_CAST_CHUNKS - 1)
    src = pl.BlockSpec((None, rows, c), lambda i: (layer, chunk(i), 0))
    dst = pl.BlockSpec((rows, c), lambda i: (chunk(i), 0))
    return src, dst, jax.ShapeDtypeStruct((r, c), BF16)


def _cast_step(step, srcs, dsts):
    @pl.when(step < N_CAST_CHUNKS)
    def _():
        for s, d in zip(srcs, dsts):
            d[...] = s[...].astype(BF16)


def _cast_kernel(s_ref, d_ref):
    d_ref[...] = s_ref[...].astype(BF16)


def _cast_bf16(w_stack, layer):
    src, dst, shape = _cast_plan(w_stack, layer)
    return pl.pallas_call(
        _cast_kernel, grid=(N_CAST_CHUNKS,), in_specs=[src], out_specs=dst, out_shape=shape,
        compiler_params=pltpu.CompilerParams(dimension_semantics=("arbitrary",)),
        name="cast_bf16")(w_stack)


def _split_refs(refs, n_in, n_cast):
    bounds = [0, n_in, n_in + n_cast, n_in + n_cast + 1, n_in + 2 * n_cast + 1, len(refs)]
    ins, srcs, (o_ref,), dsts, scratch = [refs[lo:hi] for lo, hi in zip(bounds[:-1], bounds[1:])]
    return ins, srcs, o_ref, dsts, scratch


def _call(kern, name, tm, in_specs, args, next_weights, scratch_shapes):
    n_tok = args[0].shape[0]
    plans = [_cast_plan(w, layer) for w, layer in next_weights]
    outs = pl.pallas_call(
        kern,
        grid=(n_tok // tm,),
        in_specs=in_specs + [p[0] for p in plans],
        out_specs=[_tile_spec(tm)] + [p[1] for p in plans],
        out_shape=[jax.ShapeDtypeStruct((n_tok, D_MODEL), F32)] + [p[2] for p in plans],
        scratch_shapes=scratch_shapes,
        compiler_params=pltpu.CompilerParams(
            dimension_semantics=("arbitrary",), vmem_limit_bytes=VMEM_LIMIT_BYTES),
        name=name,
    )(*args, *[w for w, _ in next_weights])
    return outs[0], outs[1:]


def _make_ffn_kernel(final, n_cast):
    n_in = 5 if final else 4

    def kern(*refs):
        ins, srcs, o_ref, dsts, (a_ref,) = _split_refs(refs, n_in, n_cast)
        x_ref, g_ref, win_ref, wout_ref = ins[:4]
        _cast_step(pl.program_id(0), srcs, dsts)
        x = x_ref[...]
        xg, rinv = _norm_parts(x, g_ref[...])
        for c in range(N_FF_CHUNKS):
            cols = slice(c * FF_CHUNK, (c + 1) * FF_CHUNK)
            up_cols = slice(D_FF + c * FF_CHUNK, D_FF + (c + 1) * FF_CHUNK)
            gate = jnp.dot(xg, win_ref[:, cols], preferred_element_type=F32) * rinv
            up = jnp.dot(xg, win_ref[:, up_cols], preferred_element_type=F32) * rinv
            a_ref[:, cols] = (jax.nn.silu(gate) * up).astype(BF16)
        y = jnp.dot(a_ref[...], wout_ref[...], preferred_element_type=F32)
        out = x + 0.5 * y
        if final:
            out = _rms(out, ins[4][...])
        o_ref[...] = out

    return kern


def _ffn(x2, gain, win, wout, next_weights, final_gain=None):
    in_specs = [_tile_spec(TM_FFN), _const_spec((1, D_MODEL)), _const_spec(win.shape),
                _const_spec(wout.shape)]
    args = [x2, gain.reshape(1, D_MODEL), win, wout]
    final = final_gain is not None
    if final:
        in_specs.append(_const_spec((1, D_MODEL)))
        args.append(final_gain.reshape(1, D_MODEL))
    return _call(_make_ffn_kernel(final, len(next_weights)), "ffn_final" if final else "ffn",
                 TM_FFN, in_specs, args, next_weights, [pltpu.VMEM((TM_FFN, D_FF), BF16)])


def _make_pool_conv_kernel(tiles_per_seq, n_cast):
    def kern(*refs):
        ins, srcs, o_ref, dsts, scratch = _split_refs(refs, 11, n_cast)
        (x_ref, g_ref, win_ref, pw_ref, pb_ref, ps_ref, cw_ref, cb_ref,
         lg_ref, lb_ref, wout_ref) = ins
        ubuf, s2buf, s4buf, s8buf, gsh, ybuf = scratch
        step = pl.program_id(0)
        t = step % tiles_per_seq
        rows = HALO + TM_MIX
        _cast_step(step, srcs, dsts)

        @pl.when(t == 0)
        def _():
            ubuf[0:HALO, :] = jnp.zeros((HALO, POOL_CH), F32)
            gsh[0, 0:HALO, :] = jnp.zeros((HALO, CONV_CH), F32)

        x = x_ref[...]
        xn = _rms(x, g_ref[...]).astype(BF16)
        h = jnp.dot(xn, win_ref[...], preferred_element_type=F32)
        u = h[:, :POOL_CH]
        ubuf[HALO:, :] = u
        gsh[0, HALO:, :] = h[:, POOL_CH:POOL_CH + CONV_CH] * jax.nn.sigmoid(h[:, POOL_CH + CONV_CH:])

        g1 = POOL_GC
        s2buf[8:, :] = ubuf[8:, :] + ubuf[pl.ds(7, rows - 8), :]
        s4buf[16:, :] = s2buf[16:, g1:] + s2buf[pl.ds(14, rows - 16), g1:]
        s8buf[24:, :] = s4buf[24:, g1:] + s4buf[pl.ds(20, rows - 24), g1:]
        s16 = s8buf[HALO:, g1:] + s8buf[pl.ds(HALO - 8, TM_MIX), g1:]
        sums = [s2buf[HALO:, :g1], s4buf[HALO:, :g1], s8buf[HALO:, :g1], s16]
        pos = t * TM_MIX + lax.broadcasted_iota(jnp.int32, (TM_MIX, 1), 0)
        pooled = []
        for gi, win in enumerate(POOL_WINDOWS):
            count = jnp.minimum(pos + 1, win).astype(F32)
            pooled.append(sums[gi] / count - u[:, gi * POOL_GC:(gi + 1) * POOL_GC])
        ya = []
        for p in range(2):
            pp = jnp.concatenate(pooled[2 * p:2 * p + 2], axis=-1).astype(BF16)
            ya.append(jnp.dot(pp, pw_ref[p], preferred_element_type=F32))
        ya = (jnp.concatenate(ya, axis=-1) + pb_ref[...]) * ps_ref[...]

        for r in range(1, 8):
            gsh[r, 0:CONV_SH_ROWS, :] = gsh[0, pl.ds(r, CONV_SH_ROWS), :]

        taps_by_shift = [[] for _ in range(8)]
        for k in range(CONV_WIDTH):
            q, r = divmod(HALO - (CONV_WIDTH - 1) + k, 8)
            taps_by_shift[r].append((k, q))

        def conv_rows(rb, carry):
            base = pl.multiple_of(rb * CONV_RB, CONV_RB)
            for lb in range(CONV_CH // 128):
                lanes = slice(lb * 128, (lb + 1) * 128)
                acc = None
                for r, taps in enumerate(taps_by_shift):
                    span = CONV_RB + 8 * max(q for _, q in taps)
                    slab = gsh[r, pl.ds(base, span), lanes]
                    for k, q in taps:
                        term = slab[8 * q:8 * q + CONV_RB] * cw_ref[k:k + 1, lanes]
                        acc = term if acc is None else acc + term
                ybuf[pl.ds(base, CONV_RB), lanes] = acc
            return carry

        lax.fori_loop(0, TM_MIX // CONV_RB, conv_rows, 0)
        y = ybuf[...] + cb_ref[...]
        yb = jax.nn.silu(_layer_norm(y, lg_ref[...], lb_ref[...]))

        cat = jnp.concatenate([ya.astype(BF16), yb.astype(BF16)], axis=-1)
        o_ref[...] = x + jnp.dot(cat, wout_ref[...], preferred_element_type=F32)

        ubuf[0:HALO, :] = ubuf[TM_MIX:, :]
        gsh[0, 0:HALO, :] = gsh[0, TM_MIX:, :]

    return kern


def _pool_conv_mixer(x2, t_len, gain, win, wout, pool_w, pool_b, pool_scale, conv_w, conv_b,
                     ln_g, ln_b, next_weights):
    rows = HALO + TM_MIX
    pw = pool_w.astype(BF16)
    z = jnp.zeros((POOL_GC, POOL_GC), BF16)
    pw_bd = jnp.stack([
        jnp.block([[pw[0], z], [z, pw[1]]]),
        jnp.block([[pw[2], z], [z, pw[3]]]),
    ])
    row = lambda n: _const_spec((1, n))
    in_specs = [_tile_spec(TM_MIX), row(D_MODEL), _const_spec(win.shape), _const_spec(pw_bd.shape),
                row(POOL_CH), row(POOL_CH), _const_spec((CONV_WIDTH, CONV_CH)), row(CONV_CH),
                row(CONV_CH), row(CONV_CH), _const_spec(wout.shape)]
    args = [x2, gain.reshape(1, D_MODEL), win, pw_bd, pool_b.reshape(1, POOL_CH),
            pool_scale.reshape(1, POOL_CH), conv_w, conv_b.reshape(1, CONV_CH),
            ln_g.reshape(1, CONV_CH), ln_b.reshape(1, CONV_CH), wout]
    scratch = [pltpu.VMEM((rows, POOL_CH), F32),
               pltpu.VMEM((rows, POOL_CH), F32),
               pltpu.VMEM((rows, POOL_CH - POOL_GC), F32),
               pltpu.VMEM((rows, POOL_CH - 2 * POOL_GC), F32),
               pltpu.VMEM((8, rows, CONV_CH), F32),
               pltpu.VMEM((TM_MIX, CONV_CH), F32)]
    return _call(_make_pool_conv_kernel(t_len // TM_MIX, len(next_weights)), "pool_conv_mixer",
                 TM_MIX, in_specs, args, next_weights, scratch)


def _make_sgu_kernel(n_cast):
    def kern(*refs):
        ins, srcs, o_ref, dsts, (vo_ref,) = _split_refs(refs, 8, n_cast)
        x_ref, g_ref, win_ref, lg_ref, lb_ref, ws_ref, bs_ref, wout_ref = ins
        n_chunks = TM_SGU // CHUNK
        _cast_step(pl.program_id(0), srcs, dsts)

        x = x_ref[...]
        xn = _rms(x, g_ref[...]).astype(BF16)
        z = _gelu(jnp.dot(xn, win_ref[...], preferred_element_type=F32))
        u = z[:, :SGU_CH]
        v = _layer_norm(z[:, SGU_CH:], lg_ref[...], lb_ref[...]).astype(BF16)

        row = lax.broadcasted_iota(jnp.int32, (CHUNK, CHUNK), 0)
        col = lax.broadcasted_iota(jnp.int32, (CHUNK, CHUNK), 1)
        mask = (col <= row).astype(F32)
        for hd in range(SGU_HEADS):
            lanes = slice(hd * SGU_HC, (hd + 1) * SGU_HC)
            w = (ws_ref[hd] * mask).astype(BF16)
            rhs = jnp.concatenate([v[n * CHUNK:(n + 1) * CHUNK, lanes] for n in range(n_chunks)], axis=-1)
            r = jnp.dot(w, rhs, preferred_element_type=F32)
            for n in range(n_chunks):
                vo_ref[n * CHUNK:(n + 1) * CHUNK, lanes] = r[:, n * SGU_HC:(n + 1) * SGU_HC]
        vo = vo_ref[...].reshape(n_chunks, CHUNK, SGU_CH) + bs_ref[...][None]
        gated = (u * vo.reshape(TM_SGU, SGU_CH)).astype(BF16)
        o_ref[...] = x + jnp.dot(gated, wout_ref[...], preferred_element_type=F32)

    return kern


def _sgu_mixer(x2, gain, win, wout, ln_g, ln_b, w_s, b_s, next_weights):
    bias = jnp.repeat(b_s.T, SGU_HC, axis=1)
    in_specs = [_tile_spec(TM_SGU), _const_spec((1, D_MODEL)), _const_spec(win.shape),
                _const_spec((1, SGU_CH)), _const_spec((1, SGU_CH)),
                _const_spec((SGU_HEADS, CHUNK, CHUNK)), _const_spec((CHUNK, SGU_CH)),
                _const_spec(wout.shape)]
    args = [x2, gain.reshape(1, D_MODEL), win, ln_g.reshape(1, SGU_CH), ln_b.reshape(1, SGU_CH),
            w_s, bias, wout]
    return _call(_make_sgu_kernel(len(next_weights)), "sgu_mixer", TM_SGU, in_specs, args,
                 next_weights, [pltpu.VMEM((TM_SGU, SGU_CH), F32)])


def kernel(x, ffn1_norm, ffn1_w_in, ffn1_w_out, mix_norm, ffn2_norm, ffn2_w_in, ffn2_w_out,
           ab_w_in, pool_w, pool_b, pool_scale, conv_w, conv_b, conv_ln_g, conv_ln_b, ab_w_out,
           sgu_w_in, sgu_ln_g, sgu_ln_b, sgu_w, sgu_b, sgu_w_out, final_norm):
    bsz, t_len, d = x.shape
    depth = ffn1_norm.shape[0]
    assert d == D_MODEL and t_len % max(TM_MIX, TM_SGU, TM_FFN) == 0
    assert (bsz * t_len) // max(TM_MIX, TM_SGU, TM_FFN) >= N_CAST_CHUNKS
    x2 = x.reshape(bsz * t_len, d)
    w = [_cast_bf16(ffn1_w_in, 0), _cast_bf16(ffn1_w_out, 0)]
    for i in range(depth):
        j = i // 2
        pool_layer = i % 2 == 0
        mixer_weights = [(ab_w_in, j), (ab_w_out, j)] if pool_layer else [(sgu_w_in, j), (sgu_w_out, j)]
        x2, w = _ffn(x2, ffn1_norm[i], w[0], w[1], mixer_weights)
        ffn2_weights = [(ffn2_w_in, i), (ffn2_w_out, i)]
        if pool_layer:
            x2, w = _pool_conv_mixer(x2, t_len, mix_norm[i], w[0], w[1], pool_w[j], pool_b[j],
                                     pool_scale[j], conv_w[j], conv_b[j], conv_ln_g[j], conv_ln_b[j],
                                     ffn2_weights)
        else:
            x2, w = _sgu_mixer(x2, mix_norm[i], w[0], w[1], sgu_ln_g[j], sgu_ln_b[j], sgu_w[j],
                               sgu_b[j], ffn2_weights)
        last = i == depth - 1
        next_ffn1 = [] if last else [(ffn1_w_in, i + 1), (ffn1_w_out, i + 1)]
        x2, w = _ffn(x2, ffn2_norm[i], w[0], w[1], next_ffn1, final_norm if last else None)
    return x2.reshape(bsz, t_len, d)
```

```python
import jax
import jax.numpy as jnp
from jax import lax
from jax.experimental import pallas as pl
from jax.experimental.pallas import tpu as pltpu

D_MODEL = 1024
D_FF = 2816
POOL_CH = 512
POOL_WINDOWS = (2, 4, 8, 16)
POOL_GC = 128
CONV_CH = 512
CONV_WIDTH = 31
AB_IN = POOL_CH + 2 * CONV_CH
SGU_CH = 1024
SGU_HEADS = 8
SGU_HC = 128
CHUNK = 128
EPS = 1e-6

MXU_TILE = 256
FF_CHUNK = MXU_TILE
N_FF_CHUNKS = D_FF // FF_CHUNK

TM_FFN = 1024
TM_MIX = 512
TM_SGU = 1024

HALO = 32
CONV_RB = 64
CONV_SH_ROWS = TM_MIX + 24

N_CAST_CHUNKS = 16
SUBLANES = 8
BF16_SUBLANES = 16

VMEM_LIMIT_BYTES = 56 * 1024 * 1024

F32 = jnp.float32
BF16 = jnp.bfloat16


def _rms(x, g):
    return x * lax.rsqrt(jnp.mean(x * x, axis=-1, keepdims=True) + EPS) * g


def _norm_parts(x, g):
    rinv = lax.rsqrt(jnp.mean(x * x, axis=-1, keepdims=True) + EPS)
    return (x * g).astype(BF16), rinv


def _layer_norm(x, g, b):
    mu = jnp.mean(x, axis=-1, keepdims=True)
    xc = x - mu
    var = jnp.mean(xc * xc, axis=-1, keepdims=True)
    return xc * lax.rsqrt(var + EPS) * g + b


def _sublane_groups(a):
    rows, lanes = a.shape
    return a.reshape(rows // SUBLANES, SUBLANES, lanes)


def _rows_later(a, d):
    a3 = _sublane_groups(a)
    rolled = pltpu.roll(a3, SUBLANES - d, axis=1)
    sub = lax.broadcasted_iota(jnp.int32, (1, SUBLANES, 1), 1)
    out = jnp.where(sub < SUBLANES - d, rolled[:-1], rolled[1:])
    return out.reshape(a.shape[0] - SUBLANES, a.shape[1])


def _rows_earlier(a, d):
    a3 = _sublane_groups(a)
    rolled = pltpu.roll(a3, d, axis=1)
    sub = lax.broadcasted_iota(jnp.int32, (1, SUBLANES, 1), 1)
    out = jnp.where(sub >= d, rolled[1:], rolled[:-1])
    return out.reshape(a.shape[0] - SUBLANES, a.shape[1])


def _gelu(x):
    return 0.5 * x * (1.0 + lax.erf(x * (2.0 ** -0.5)))


def _const_spec(shape):
    nd = len(shape)
    return pl.BlockSpec(shape, lambda *_: (0,) * nd, pipeline_mode=pl.Buffered(1))


def _tile_spec(tm):
    return pl.BlockSpec((tm, D_MODEL), lambda i: (i, 0))


def _cast_plan(w_stack, layer):
    _, r, c = w_stack.shape
    assert r % (N_CAST_CHUNKS * BF16_SUBLANES) == 0
    rows = r // N_CAST_CHUNKS
    chunk = lambda i: jnp.minimum(i, N_CAST_CHUNKS - 1)
    src = pl.BlockSpec((None, rows, c), lambda i: (layer, chunk(i), 0))
    dst = pl.BlockSpec((rows, c), lambda i: (chunk(i), 0))
    return src, dst, jax.ShapeDtypeStruct((r, c), BF16)


def _cast_step(step, srcs, dsts):
    @pl.when(step < N_CAST_CHUNKS)
    def _():
        for s, d in zip(srcs, dsts):
            d[...] = s[...].astype(BF16)


def _cast_kernel(s_ref, d_ref):
    d_ref[...] = s_ref[...].astype(BF16)


def _cast_bf16(w_stack, layer):
    src, dst, shape = _cast_plan(w_stack, layer)
    return pl.pallas_call(
        _cast_kernel, grid=(N_CAST_CHUNKS,), in_specs=[src], out_specs=dst, out_shape=shape,
        compiler_params=pltpu.CompilerParams(dimension_semantics=("arbitrary",)),
        name="cast_bf16")(w_stack)


def _split_refs(refs, n_in, n_cast):
    bounds = [0, n_in, n_in + n_cast, n_in + n_cast + 1, n_in + 2 * n_cast + 1, len(refs)]
    ins, srcs, (o_ref,), dsts, scratch = [refs[lo:hi] for lo, hi in zip(bounds[:-1], bounds[1:])]
    return ins, srcs, o_ref, dsts, scratch


def _call(kern, name, tm, in_specs, args, next_weights, scratch_shapes):
    n_tok = args[0].shape[0]
    plans = [_cast_plan(w, layer) for w, layer in next_weights]
    outs = pl.pallas_call(
        kern,
        grid=(n_tok // tm,),
        in_specs=in_specs + [p[0] for p in plans],
        out_specs=[_tile_spec(tm)] + [p[1] for p in plans],
        out_shape=[jax.ShapeDtypeStruct((n_tok, D_MODEL), F32)] + [p[2] for p in plans],
        scratch_shapes=scratch_shapes,
        compiler_params=pltpu.CompilerParams(
            dimension_semantics=("arbitrary",), vmem_limit_bytes=VMEM_LIMIT_BYTES),
        name=name,
    )(*args, *[w for w, _ in next_weights])
    return outs[0], outs[1:]


def _make_ffn_kernel(final, n_cast):
    n_in = 5 if final else 4

    def kern(*refs):
        ins, srcs, o_ref, dsts, (a_ref,) = _split_refs(refs, n_in, n_cast)
        x_ref, g_ref, win_ref, wout_ref = ins[:4]
        _cast_step(pl.program_id(0), srcs, dsts)
        x = x_ref[...]
        xg, rinv = _norm_parts(x, g_ref[...])
        for c in range(N_FF_CHUNKS):
            cols = slice(c * FF_CHUNK, (c + 1) * FF_CHUNK)
            up_cols = slice(D_FF + c * FF_CHUNK, D_FF + (c + 1) * FF_CHUNK)
            gate = jnp.dot(xg, win_ref[:, cols], preferred_element_type=F32) * rinv
            up = jnp.dot(xg, win_ref[:, up_cols], preferred_element_type=F32) * rinv
            a_ref[:, cols] = (jax.nn.silu(gate) * up).astype(BF16)
        y = jnp.dot(a_ref[...], wout_ref[...], preferred_element_type=F32)
        out = x + 0.5 * y
        if final:
            out = _rms(out, ins[4][...])
        o_ref[...] = out

    return kern


def _ffn(x2, gain, win, wout, next_weights, final_gain=None):
    in_specs = [_tile_spec(TM_FFN), _const_spec((1, D_MODEL)), _const_spec(win.shape),
                _const_spec(wout.shape)]
    args = [x2, gain.reshape(1, D_MODEL), win, wout]
    final = final_gain is not None
    if final:
        in_specs.append(_const_spec((1, D_MODEL)))
        args.append(final_gain.reshape(1, D_MODEL))
    return _call(_make_ffn_kernel(final, len(next_weights)), "ffn_final" if final else "ffn",
                 TM_FFN, in_specs, args, next_weights, [pltpu.VMEM((TM_FFN, D_FF), BF16)])


def _make_pool_conv_kernel(tiles_per_seq, n_cast):
    def kern(*refs):
        ins, srcs, o_ref, dsts, scratch = _split_refs(refs, 11, n_cast)
        (x_ref, g_ref, win_ref, pw_ref, pb_ref, ps_ref, cw_ref, cb_ref,
         lg_ref, lb_ref, wout_ref) = ins
        ubuf, s2buf, s4buf, s8buf, gsh, ybuf = scratch
        step = pl.program_id(0)
        t = step % tiles_per_seq
        rows = HALO + TM_MIX
        _cast_step(step, srcs, dsts)

        @pl.when(t == 0)
        def _():
            ubuf[0:HALO, :] = jnp.zeros((HALO, POOL_CH), F32)
            gsh[0, 0:HALO, :] = jnp.zeros((HALO, CONV_CH), F32)

        x = x_ref[...]
        xn = _rms(x, g_ref[...]).astype(BF16)
        h = jnp.dot(xn, win_ref[...], preferred_element_type=F32)
        u = h[:, :POOL_CH]
        ubuf[HALO:, :] = u
        gsh[0, HALO:, :] = h[:, POOL_CH:POOL_CH + CONV_CH] * jax.nn.sigmoid(h[:, POOL_CH + CONV_CH:])

        g1 = POOL_GC
        s2buf[8:, :] = ubuf[8:, :] + _rows_earlier(ubuf[...], 1)
        s4buf[16:, :] = s2buf[16:, g1:] + _rows_earlier(s2buf[8:, g1:], 2)
        s8buf[24:, :] = s4buf[24:, g1:] + _rows_earlier(s4buf[16:, g1:], 4)
        s16 = s8buf[HALO:, g1:] + s8buf[pl.ds(HALO - 8, TM_MIX), g1:]
        sums = [s2buf[HALO:, :g1], s4buf[HALO:, :g1], s8buf[HALO:, :g1], s16]
        pos = t * TM_MIX + lax.broadcasted_iota(jnp.int32, (TM_MIX, 1), 0)
        pooled = []
        for gi, win in enumerate(POOL_WINDOWS):
            count = jnp.minimum(pos + 1, win).astype(F32)
            pooled.append(sums[gi] / count - u[:, gi * POOL_GC:(gi + 1) * POOL_GC])
        ya = []
        for p in range(2):
            pp = jnp.concatenate(pooled[2 * p:2 * p + 2], axis=-1).astype(BF16)
            ya.append(jnp.dot(pp, pw_ref[p], preferred_element_type=F32))
        ya = (jnp.concatenate(ya, axis=-1) + pb_ref[...]) * ps_ref[...]

        g_all = gsh[0]
        for r in range(1, 8):
            gsh[r, 0:CONV_SH_ROWS, :] = _rows_later(g_all, r)[0:CONV_SH_ROWS]

        taps_by_shift = [[] for _ in range(8)]
        for k in range(CONV_WIDTH):
            q, r = divmod(HALO - (CONV_WIDTH - 1) + k, 8)
            taps_by_shift[r].append((k, q))

        def conv_rows(rb, carry):
            base = pl.multiple_of(rb * CONV_RB, CONV_RB)
            for lb in range(CONV_CH // 128):
                lanes = slice(lb * 128, (lb + 1) * 128)
                acc = None
                for r, taps in enumerate(taps_by_shift):
                    span = CONV_RB + 8 * max(q for _, q in taps)
                    slab = gsh[r, pl.ds(base, span), lanes]
                    for k, q in taps:
                        term = slab[8 * q:8 * q + CONV_RB] * cw_ref[k:k + 1, lanes]
                        acc = term if acc is None else acc + term
                ybuf[pl.ds(base, CONV_RB), lanes] = acc
            return carry

        lax.fori_loop(0, TM_MIX // CONV_RB, conv_rows, 0)
        y = ybuf[...] + cb_ref[...]
        yb = jax.nn.silu(_layer_norm(y, lg_ref[...], lb_ref[...]))

        cat = jnp.concatenate([ya.astype(BF16), yb.astype(BF16)], axis=-1)
        o_ref[...] = x + jnp.dot(cat, wout_ref[...], preferred_element_type=F32)

        ubuf[0:HALO, :] = ubuf[TM_MIX:, :]
        gsh[0, 0:HALO, :] = gsh[0, TM_MIX:, :]

    return kern


def _pool_conv_mixer(x2, t_len, gain, win, wout, pool_w, pool_b, pool_scale, conv_w, conv_b,
                     ln_g, ln_b, next_weights):
    rows = HALO + TM_MIX
    pw = pool_w.astype(BF16)
    z = jnp.zeros((POOL_GC, POOL_GC), BF16)
    pw_bd = jnp.stack([
        jnp.block([[pw[0], z], [z, pw[1]]]),
        jnp.block([[pw[2], z], [z, pw[3]]]),
    ])
    row = lambda n: _const_spec((1, n))
    in_specs = [_tile_spec(TM_MIX), row(D_MODEL), _const_spec(win.shape), _const_spec(pw_bd.shape),
                row(POOL_CH), row(POOL_CH), _const_spec((CONV_WIDTH, CONV_CH)), row(CONV_CH),
                row(CONV_CH), row(CONV_CH), _const_spec(wout.shape)]
    args = [x2, gain.reshape(1, D_MODEL), win, pw_bd, pool_b.reshape(1, POOL_CH),
            pool_scale.reshape(1, POOL_CH), conv_w, conv_b.reshape(1, CONV_CH),
            ln_g.reshape(1, CONV_CH), ln_b.reshape(1, CONV_CH), wout]
    scratch = [pltpu.VMEM((rows, POOL_CH), F32),
               pltpu.VMEM((rows, POOL_CH), F32),
               pltpu.VMEM((rows, POOL_CH - POOL_GC), F32),
               pltpu.VMEM((rows, POOL_CH - 2 * POOL_GC), F32),
               pltpu.VMEM((8, rows, CONV_CH), F32),
               pltpu.VMEM((TM_MIX, CONV_CH), F32)]
    return _call(_make_pool_conv_kernel(t_len // TM_MIX, len(next_weights)), "pool_conv_mixer",
                 TM_MIX, in_specs, args, next_weights, scratch)


def _make_sgu_kernel(n_cast):
    def kern(*refs):
        ins, srcs, o_ref, dsts, (zv_ref, u_ref, v_ref) = _split_refs(refs, 8, n_cast)
        x_ref, g_ref, win_ref, lg_ref, lb_ref, ws_ref, bs_ref, wout_ref = ins
        n_chunks = TM_SGU // CHUNK
        n_cols = SGU_CH // MXU_TILE
        _cast_step(pl.program_id(0), srcs, dsts)

        x = x_ref[...]
        xn = _rms(x, g_ref[...]).astype(BF16)
        for c in range(n_cols):
            cols = slice(c * MXU_TILE, (c + 1) * MXU_TILE)
            w_cols = slice(SGU_CH + c * MXU_TILE, SGU_CH + (c + 1) * MXU_TILE)
            zv_ref[:, cols] = _gelu(jnp.dot(xn, win_ref[:, w_cols], preferred_element_type=F32))
        for c in range(n_cols):
            cols = slice(c * MXU_TILE, (c + 1) * MXU_TILE)
            u_ref[:, cols] = _gelu(jnp.dot(xn, win_ref[:, cols], preferred_element_type=F32))
        v_ref[...] = _layer_norm(zv_ref[...], lg_ref[...], lb_ref[...]).astype(BF16)

        row = lax.broadcasted_iota(jnp.int32, (CHUNK, CHUNK), 0)
        col = lax.broadcasted_iota(jnp.int32, (CHUNK, CHUNK), 1)
        mask = (col <= row).astype(F32)
        heads_per_tile = MXU_TILE // SGU_HC
        acc = x
        for p in range(n_cols):
            cols = slice(p * MXU_TILE, (p + 1) * MXU_TILE)
            vo = []
            for hd in range(p * heads_per_tile, (p + 1) * heads_per_tile):
                lanes = slice(hd * SGU_HC, (hd + 1) * SGU_HC)
                w = (ws_ref[hd] * mask).astype(BF16)
                rhs = jnp.concatenate(
                    [v_ref[n * CHUNK:(n + 1) * CHUNK, lanes] for n in range(n_chunks)], axis=-1)
                r = jnp.dot(w, rhs, preferred_element_type=F32)
                vo.append(jnp.concatenate(
                    [r[:, n * SGU_HC:(n + 1) * SGU_HC] for n in range(n_chunks)], axis=0))
            vo = jnp.concatenate(vo, axis=-1).reshape(n_chunks, CHUNK, MXU_TILE) + bs_ref[:, cols][None]
            gated = (u_ref[:, cols] * vo.reshape(TM_SGU, MXU_TILE)).astype(BF16)
            acc = acc + jnp.dot(gated, wout_ref[cols, :], preferred_element_type=F32)
        o_ref[...] = acc

    return kern


def _sgu_mixer(x2, gain, win, wout, ln_g, ln_b, w_s, b_s, next_weights):
    bias = jnp.repeat(b_s.T, SGU_HC, axis=1)
    in_specs = [_tile_spec(TM_SGU), _const_spec((1, D_MODEL)), _const_spec(win.shape),
                _const_spec((1, SGU_CH)), _const_spec((1, SGU_CH)),
                _const_spec((SGU_HEADS, CHUNK, CHUNK)), _const_spec((CHUNK, SGU_CH)),
                _const_spec(wout.shape)]
    args = [x2, gain.reshape(1, D_MODEL), win, ln_g.reshape(1, SGU_CH), ln_b.reshape(1, SGU_CH),
            w_s, bias, wout]
    scratch = [pltpu.VMEM((TM_SGU, SGU_CH), F32),
               pltpu.VMEM((TM_SGU, SGU_CH), F32),
               pltpu.VMEM((TM_SGU, SGU_CH), BF16)]
    return _call(_make_sgu_kernel(len(next_weights)), "sgu_mixer", TM_SGU, in_specs, args,
                 next_weights, scratch)


def kernel(x, ffn1_norm, ffn1_w_in, ffn1_w_out, mix_norm, ffn2_norm, ffn2_w_in, ffn2_w_out,
           ab_w_in, pool_w, pool_b, pool_scale, conv_w, conv_b, conv_ln_g, conv_ln_b, ab_w_out,
           sgu_w_in, sgu_ln_g, sgu_ln_b, sgu_w, sgu_b, sgu_w_out, final_norm):
    bsz, t_len, d = x.shape
    depth = ffn1_norm.shape[0]
    assert d == D_MODEL and t_len % max(TM_MIX, TM_SGU, TM_FFN) == 0
    assert (bsz * t_len) // max(TM_MIX, TM_SGU, TM_FFN) >= N_CAST_CHUNKS
    x2 = x.reshape(bsz * t_len, d)
    w = [_cast_bf16(ffn1_w_in, 0), _cast_bf16(ffn1_w_out, 0)]
    for i in range(depth):
        j = i // 2
        pool_layer = i % 2 == 0
        mixer_weights = [(ab_w_in, j), (ab_w_out, j)] if pool_layer else [(sgu_w_in, j), (sgu_w_out, j)]
        x2, w = _ffn(x2, ffn1_norm[i], w[0], w[1], mixer_weights)
        ffn2_weights = [(ffn2_w_in, i), (ffn2_w_out, i)]
        if pool_layer:
            x2, w = _pool_conv_mixer(x2, t_len, mix_norm[i], w[0], w[1], pool_w[j], pool_b[j],
                                     pool_scale[j], conv_w[j], conv_b[j], conv_ln_g[j], conv_ln_b[j],
                                     ffn2_weights)
        else:
            x2, w = _sgu_mixer(x2, mix_norm[i], w[0], w[1], sgu_ln_g[j], sgu_ln_b[j], sgu_w[j],
                               sgu_b[j], ffn2_weights)
        last = i == depth - 1
        next_ffn1 = [] if last else [(ffn1_w_in, i + 1), (ffn1_w_out, i + 1)]
        x2, w = _ffn(x2, ffn2_norm[i], w[0], w[1], next_ffn1, final_norm if last else None)
    return x2.reshape(bsz, t_len, d)
```

```python
import jax
import jax.numpy as jnp
from jax import lax
from jax.experimental import pallas as pl
from jax.experimental.pallas import tpu as pltpu

D_MODEL = 1024
D_FF = 2816
POOL_CH = 512
POOL_WINDOWS = (2, 4, 8, 16)
POOL_GC = 128
CONV_CH = 512
CONV_WIDTH = 31
AB_IN = POOL_CH + 2 * CONV_CH
SGU_CH = 1024
SGU_HEADS = 8
SGU_HC = 128
CHUNK = 128
EPS = 1e-6

MXU_TILE = 256
FF_CHUNK = MXU_TILE
N_FF_CHUNKS = D_FF // FF_CHUNK

TM_FFN = 1024
TM_MIX = 512
TM_SGU = 512

HALO = 32
CONV_RB = 64
CONV_SH_ROWS = TM_MIX + 24

N_CAST_CHUNKS = 16
N_CAST_CHUNKS_ALONE = 4
SUBLANES = 8
BF16_SUBLANES = 16

VMEM_LIMIT_BYTES = 56 * 1024 * 1024

F32 = jnp.float32
BF16 = jnp.bfloat16


def _rms(x, g):
    return x * lax.rsqrt(jnp.mean(x * x, axis=-1, keepdims=True) + EPS) * g


def _norm_parts(x, g):
    rinv = lax.rsqrt(jnp.mean(x * x, axis=-1, keepdims=True) + EPS)
    return (x * g).astype(BF16), rinv


def _layer_norm(x, g, b):
    mu = jnp.mean(x, axis=-1, keepdims=True)
    xc = x - mu
    var = jnp.mean(xc * xc, axis=-1, keepdims=True)
    return xc * lax.rsqrt(var + EPS) * g + b


def _sublane_groups(a):
    rows, lanes = a.shape
    return a.reshape(rows // SUBLANES, SUBLANES, lanes)


def _rows_later(a, d):
    a3 = _sublane_groups(a)
    rolled = pltpu.roll(a3, SUBLANES - d, axis=1)
    sub = lax.broadcasted_iota(jnp.int32, (1, SUBLANES, 1), 1)
    out = jnp.where(sub < SUBLANES - d, rolled[:-1], rolled[1:])
    return out.reshape(a.shape[0] - SUBLANES, a.shape[1])


def _rows_earlier(a, d):
    a3 = _sublane_groups(a)
    rolled = pltpu.roll(a3, d, axis=1)
    sub = lax.broadcasted_iota(jnp.int32, (1, SUBLANES, 1), 1)
    out = jnp.where(sub >= d, rolled[1:], rolled[:-1])
    return out.reshape(a.shape[0] - SUBLANES, a.shape[1])


def _gelu(x):
    return 0.5 * x * (1.0 + lax.erf(x * (2.0 ** -0.5)))


def _const_spec(shape):
    nd = len(shape)
    return pl.BlockSpec(shape, lambda *_: (0,) * nd, pipeline_mode=pl.Buffered(1))


def _tile_spec(tm):
    return pl.BlockSpec((tm, D_MODEL), lambda i: (i, 0))


def _cast_plan(w_stack, layer, n_chunks=N_CAST_CHUNKS):
    _, r, c = w_stack.shape
    assert r % (n_chunks * BF16_SUBLANES) == 0
    rows = r // n_chunks
    chunk = lambda i: jnp.minimum(i, n_chunks - 1)
    src = pl.BlockSpec((None, rows, c), lambda i: (layer, chunk(i), 0))
    dst = pl.BlockSpec((rows, c), lambda i: (chunk(i), 0))
    return src, dst, jax.ShapeDtypeStruct((r, c), BF16)


def _cast_step(step, srcs, dsts):
    @pl.when(step < N_CAST_CHUNKS)
    def _():
        for s, d in zip(srcs, dsts):
            d[...] = s[...].astype(BF16)


def _cast_kernel(s_ref, d_ref):
    d_ref[...] = s_ref[...].astype(BF16)


def _cast_bf16(w_stack, layer):
    src, dst, shape = _cast_plan(w_stack, layer, N_CAST_CHUNKS_ALONE)
    return pl.pallas_call(
        _cast_kernel, grid=(N_CAST_CHUNKS_ALONE,), in_specs=[src], out_specs=dst, out_shape=shape,
        compiler_params=pltpu.CompilerParams(
            dimension_semantics=("arbitrary",), vmem_limit_bytes=VMEM_LIMIT_BYTES),
        name="cast_bf16")(w_stack)


def _split_refs(refs, n_in, n_cast):
    bounds = [0, n_in, n_in + n_cast, n_in + n_cast + 1, n_in + 2 * n_cast + 1, len(refs)]
    ins, srcs, (o_ref,), dsts, scratch = [refs[lo:hi] for lo, hi in zip(bounds[:-1], bounds[1:])]
    return ins, srcs, o_ref, dsts, scratch


def _call(kern, name, tm, in_specs, args, next_weights, scratch_shapes):
    n_tok = args[0].shape[0]
    plans = [_cast_plan(w, layer) for w, layer in next_weights]
    outs = pl.pallas_call(
        kern,
        grid=(n_tok // tm,),
        in_specs=in_specs + [p[0] for p in plans],
        out_specs=[_tile_spec(tm)] + [p[1] for p in plans],
        out_shape=[jax.ShapeDtypeStruct((n_tok, D_MODEL), F32)] + [p[2] for p in plans],
        scratch_shapes=scratch_shapes,
        compiler_params=pltpu.CompilerParams(
            dimension_semantics=("arbitrary",), vmem_limit_bytes=VMEM_LIMIT_BYTES),
        name=name,
    )(*args, *[w for w, _ in next_weights])
    return outs[0], outs[1:]


def _make_ffn_kernel(final, n_cast):
    n_in = 5 if final else 4

    def kern(*refs):
        ins, srcs, o_ref, dsts, (a_ref,) = _split_refs(refs, n_in, n_cast)
        x_ref, g_ref, win_ref, wout_ref = ins[:4]
        _cast_step(pl.program_id(0), srcs, dsts)
        x = x_ref[...]
        xg, rinv = _norm_parts(x, g_ref[...])
        for c in range(N_FF_CHUNKS):
            cols = slice(c * FF_CHUNK, (c + 1) * FF_CHUNK)
            up_cols = slice(D_FF + c * FF_CHUNK, D_FF + (c + 1) * FF_CHUNK)
            gate = jnp.dot(xg, win_ref[:, cols], preferred_element_type=F32) * rinv
            up = jnp.dot(xg, win_ref[:, up_cols], preferred_element_type=F32) * rinv
            a_ref[:, cols] = (jax.nn.silu(gate) * up).astype(BF16)
        y = jnp.dot(a_ref[...], wout_ref[...], preferred_element_type=F32)
        out = x + 0.5 * y
        if final:
            out = _rms(out, ins[4][...])
        o_ref[...] = out

    return kern


def _ffn(x2, gain, win, wout, next_weights, final_gain=None):
    in_specs = [_tile_spec(TM_FFN), _const_spec((1, D_MODEL)), _const_spec(win.shape),
                _const_spec(wout.shape)]
    args = [x2, gain.reshape(1, D_MODEL), win, wout]
    final = final_gain is not None
    if final:
        in_specs.append(_const_spec((1, D_MODEL)))
        args.append(final_gain.reshape(1, D_MODEL))
    return _call(_make_ffn_kernel(final, len(next_weights)), "ffn_final" if final else "ffn",
                 TM_FFN, in_specs, args, next_weights, [pltpu.VMEM((TM_FFN, D_FF), BF16)])


def _make_pool_conv_kernel(tiles_per_seq, n_cast):
    def kern(*refs):
        ins, srcs, o_ref, dsts, scratch = _split_refs(refs, 11, n_cast)
        (x_ref, g_ref, win_ref, pw_ref, pb_ref, ps_ref, cw_ref, cb_ref,
         lg_ref, lb_ref, wout_ref) = ins
        ubuf, s2buf, s4buf, s8buf, gsh, ybuf = scratch
        step = pl.program_id(0)
        t = step % tiles_per_seq
        rows = HALO + TM_MIX
        _cast_step(step, srcs, dsts)

        @pl.when(t == 0)
        def _():
            ubuf[0:HALO, :] = jnp.zeros((HALO, POOL_CH), F32)
            gsh[0, 0:HALO, :] = jnp.zeros((HALO, CONV_CH), F32)

        x = x_ref[...]
        xn = _rms(x, g_ref[...]).astype(BF16)
        h = jnp.dot(xn, win_ref[...], preferred_element_type=F32)
        u = h[:, :POOL_CH]
        ubuf[HALO:, :] = u
        gsh[0, HALO:, :] = h[:, POOL_CH:POOL_CH + CONV_CH] * jax.nn.sigmoid(h[:, POOL_CH + CONV_CH:])

        g1 = POOL_GC
        s2buf[8:, :] = ubuf[8:, :] + _rows_earlier(ubuf[...], 1)
        s4buf[16:, :] = s2buf[16:, g1:] + _rows_earlier(s2buf[8:, g1:], 2)
        s8buf[24:, :] = s4buf[24:, g1:] + _rows_earlier(s4buf[16:, g1:], 4)
        s16 = s8buf[HALO:, g1:] + s8buf[pl.ds(HALO - 8, TM_MIX), g1:]
        sums = [s2buf[HALO:, :g1], s4buf[HALO:, :g1], s8buf[HALO:, :g1], s16]
        pos = t * TM_MIX + lax.broadcasted_iota(jnp.int32, (TM_MIX, 1), 0)
        pooled = []
        for gi, win in enumerate(POOL_WINDOWS):
            count = jnp.minimum(pos + 1, win).astype(F32)
            pooled.append(sums[gi] / count - u[:, gi * POOL_GC:(gi + 1) * POOL_GC])
        ya = []
        for p in range(2):
            pp = jnp.concatenate(pooled[2 * p:2 * p + 2], axis=-1).astype(BF16)
            ya.append(jnp.dot(pp, pw_ref[p], preferred_element_type=F32))
        ya = (jnp.concatenate(ya, axis=-1) + pb_ref[...]) * ps_ref[...]

        g_all = gsh[0]
        for r in range(1, 8):
            gsh[r, 0:CONV_SH_ROWS, :] = _rows_later(g_all, r)[0:CONV_SH_ROWS]

        taps_by_shift = [[] for _ in range(8)]
        for k in range(CONV_WIDTH):
            q, r = divmod(HALO - (CONV_WIDTH - 1) + k, 8)
            taps_by_shift[r].append((k, q))

        def conv_rows(rb, carry):
            base = pl.multiple_of(rb * CONV_RB, CONV_RB)
            for lb in range(CONV_CH // 128):
                lanes = slice(lb * 128, (lb + 1) * 128)
                acc = None
                for r, taps in enumerate(taps_by_shift):
                    span = CONV_RB + 8 * max(q for _, q in taps)
                    slab = gsh[r, pl.ds(base, span), lanes]
                    for k, q in taps:
                        term = slab[8 * q:8 * q + CONV_RB] * cw_ref[k:k + 1, lanes]
                        acc = term if acc is None else acc + term
                ybuf[pl.ds(base, CONV_RB), lanes] = acc
            return carry

        lax.fori_loop(0, TM_MIX // CONV_RB, conv_rows, 0)
        y = ybuf[...] + cb_ref[...]
        yb = jax.nn.silu(_layer_norm(y, lg_ref[...], lb_ref[...]))

        cat = jnp.concatenate([ya.astype(BF16), yb.astype(BF16)], axis=-1)
        o_ref[...] = x + jnp.dot(cat, wout_ref[...], preferred_element_type=F32)

        ubuf[0:HALO, :] = ubuf[TM_MIX:, :]
        gsh[0, 0:HALO, :] = gsh[0, TM_MIX:, :]

    return kern


def _pool_conv_mixer(x2, t_len, gain, win, wout, pool_w, pool_b, pool_scale, conv_w, conv_b,
                     ln_g, ln_b, next_weights):
    rows = HALO + TM_MIX
    pw = pool_w.astype(BF16)
    z = jnp.zeros((POOL_GC, POOL_GC), BF16)
    pw_bd = jnp.stack([
        jnp.block([[pw[0], z], [z, pw[1]]]),
        jnp.block([[pw[2], z], [z, pw[3]]]),
    ])
    row = lambda n: _const_spec((1, n))
    in_specs = [_tile_spec(TM_MIX), row(D_MODEL), _const_spec(win.shape), _const_spec(pw_bd.shape),
                row(POOL_CH), row(POOL_CH), _const_spec((CONV_WIDTH, CONV_CH)), row(CONV_CH),
                row(CONV_CH), row(CONV_CH), _const_spec(wout.shape)]
    args = [x2, gain.reshape(1, D_MODEL), win, pw_bd, pool_b.reshape(1, POOL_CH),
            pool_scale.reshape(1, POOL_CH), conv_w, conv_b.reshape(1, CONV_CH),
            ln_g.reshape(1, CONV_CH), ln_b.reshape(1, CONV_CH), wout]
    scratch = [pltpu.VMEM((rows, POOL_CH), F32),
               pltpu.VMEM((rows, POOL_CH), F32),
               pltpu.VMEM((rows, POOL_CH - POOL_GC), F32),
               pltpu.VMEM((rows, POOL_CH - 2 * POOL_GC), F32),
               pltpu.VMEM((8, rows, CONV_CH), F32),
               pltpu.VMEM((TM_MIX, CONV_CH), F32)]
    return _call(_make_pool_conv_kernel(t_len // TM_MIX, len(next_weights)), "pool_conv_mixer",
                 TM_MIX, in_specs, args, next_weights, scratch)


def _make_sgu_kernel(n_cast):
    def kern(*refs):
        ins, srcs, o_ref, dsts, (zv_ref, u_ref, v_ref) = _split_refs(refs, 8, n_cast)
        x_ref, g_ref, win_ref, lg_ref, lb_ref, ws_ref, bs_ref, wout_ref = ins
        n_chunks = TM_SGU // CHUNK
        n_cols = SGU_CH // MXU_TILE
        _cast_step(pl.program_id(0), srcs, dsts)

        x = x_ref[...]
        xn = _rms(x, g_ref[...]).astype(BF16)
        for c in range(n_cols):
            cols = slice(c * MXU_TILE, (c + 1) * MXU_TILE)
            w_cols = slice(SGU_CH + c * MXU_TILE, SGU_CH + (c + 1) * MXU_TILE)
            zv_ref[:, cols] = _gelu(jnp.dot(xn, win_ref[:, w_cols], preferred_element_type=F32))
        for c in range(n_cols):
            cols = slice(c * MXU_TILE, (c + 1) * MXU_TILE)
            u_ref[:, cols] = _gelu(jnp.dot(xn, win_ref[:, cols], preferred_element_type=F32))
        v_ref[...] = _layer_norm(zv_ref[...], lg_ref[...], lb_ref[...]).astype(BF16)

        row = lax.broadcasted_iota(jnp.int32, (CHUNK, CHUNK), 0)
        col = lax.broadcasted_iota(jnp.int32, (CHUNK, CHUNK), 1)
        mask = (col <= row).astype(F32)
        heads_per_tile = MXU_TILE // SGU_HC
        acc = x
        for p in range(n_cols):
            cols = slice(p * MXU_TILE, (p + 1) * MXU_TILE)
            vo = []
            for hd in range(p * heads_per_tile, (p + 1) * heads_per_tile):
                lanes = slice(hd * SGU_HC, (hd + 1) * SGU_HC)
                w = (ws_ref[hd] * mask).astype(BF16)
                rhs = jnp.concatenate(
                    [v_ref[n * CHUNK:(n + 1) * CHUNK, lanes] for n in range(n_chunks)], axis=-1)
                r = jnp.dot(w, rhs, preferred_element_type=F32)
                vo.append(jnp.concatenate(
                    [r[:, n * SGU_HC:(n + 1) * SGU_HC] for n in range(n_chunks)], axis=0))
            vo = jnp.concatenate(vo, axis=-1).reshape(n_chunks, CHUNK, MXU_TILE) + bs_ref[:, cols][None]
            gated = (u_ref[:, cols] * vo.reshape(TM_SGU, MXU_TILE)).astype(BF16)
            acc = acc + jnp.dot(gated, wout_ref[cols, :], preferred_element_type=F32)
        o_ref[...] = acc

    return kern


def _sgu_mixer(x2, gain, win, wout, ln_g, ln_b, w_s, b_s, next_weights):
    bias = jnp.repeat(b_s.T, SGU_HC, axis=1)
    in_specs = [_tile_spec(TM_SGU), _const_spec((1, D_MODEL)), _const_spec(win.shape),
                _const_spec((1, SGU_CH)), _const_spec((1, SGU_CH)),
                _const_spec((SGU_HEADS, CHUNK, CHUNK)), _const_spec((CHUNK, SGU_CH)),
                _const_spec(wout.shape)]
    args = [x2, gain.reshape(1, D_MODEL), win, ln_g.reshape(1, SGU_CH), ln_b.reshape(1, SGU_CH),
            w_s, bias, wout]
    scratch = [pltpu.VMEM((TM_SGU, SGU_CH), F32),
               pltpu.VMEM((TM_SGU, SGU_CH), F32),
               pltpu.VMEM((TM_SGU, SGU_CH), BF16)]
    return _call(_make_sgu_kernel(len(next_weights)), "sgu_mixer", TM_SGU, in_specs, args,
                 next_weights, scratch)


def kernel(x, ffn1_norm, ffn1_w_in, ffn1_w_out, mix_norm, ffn2_norm, ffn2_w_in, ffn2_w_out,
           ab_w_in, pool_w, pool_b, pool_scale, conv_w, conv_b, conv_ln_g, conv_ln_b, ab_w_out,
           sgu_w_in, sgu_ln_g, sgu_ln_b, sgu_w, sgu_b, sgu_w_out, final_norm):
    bsz, t_len, d = x.shape
    depth = ffn1_norm.shape[0]
    assert d == D_MODEL and t_len % max(TM_MIX, TM_SGU, TM_FFN) == 0
    assert (bsz * t_len) // max(TM_MIX, TM_SGU, TM_FFN) >= N_CAST_CHUNKS
    x2 = x.reshape(bsz * t_len, d)
    w = [_cast_bf16(ffn1_w_in, 0), _cast_bf16(ffn1_w_out, 0)]
    for i in range(depth):
        j = i // 2
        pool_layer = i % 2 == 0
        mixer_weights = [(ab_w_in, j), (ab_w_out, j)] if pool_layer else [(sgu_w_in, j), (sgu_w_out, j)]
        x2, w = _ffn(x2, ffn1_norm[i], w[0], w[1], mixer_weights)
        ffn2_weights = [(ffn2_w_in, i), (ffn2_w_out, i)]
        if pool_layer:
            x2, w = _pool_conv_mixer(x2, t_len, mix_norm[i], w[0], w[1], pool_w[j], pool_b[j],
                                     pool_scale[j], conv_w[j], conv_b[j], conv_ln_g[j], conv_ln_b[j],
                                     ffn2_weights)
        else:
            x2, w = _sgu_mixer(x2, mix_norm[i], w[0], w[1], sgu_ln_g[j], sgu_ln_b[j], sgu_w[j],
                               sgu_b[j], ffn2_weights)
        last = i == depth - 1
        next_ffn1 = [] if last else [(ffn1_w_in, i + 1), (ffn1_w_out, i + 1)]
        x2, w = _ffn(x2, ffn2_norm[i], w[0], w[1], next_ffn1, final_norm if last else None)
    return x2.reshape(bsz, t_len, d)
```

```python
import jax
import jax.numpy as jnp
from jax import lax
from jax.experimental import pallas as pl
from jax.experimental.pallas import tpu as pltpu

D_MODEL = 1024
D_FF = 2816
POOL_CH = 512
POOL_WINDOWS = (2, 4, 8, 16)
POOL_GC = 128
CONV_CH = 512
CONV_WIDTH = 31
AB_IN = POOL_CH + 2 * CONV_CH
SGU_CH = 1024
SGU_HEADS = 8
SGU_HC = 128
CHUNK = 128
EPS = 1e-6

MXU_TILE = 256
FF_CHUNK = MXU_TILE
N_FF_CHUNKS = D_FF // FF_CHUNK

TM_FFN = 1024
TM_MIX = 512
TM_SGU = 1024

HALO = 32
CONV_RB = 64
CONV_SH_ROWS = TM_MIX + 24

N_CAST_CHUNKS = 16
N_CAST_CHUNKS_ALONE = 4
SUBLANES = 8
BF16_SUBLANES = 16

VMEM_LIMIT_BYTES = 56 * 1024 * 1024

F32 = jnp.float32
BF16 = jnp.bfloat16


def _rms(x, g):
    return x * lax.rsqrt(jnp.mean(x * x, axis=-1, keepdims=True) + EPS) * g


def _norm_parts(x, g):
    rinv = lax.rsqrt(jnp.mean(x * x, axis=-1, keepdims=True) + EPS)
    return (x * g).astype(BF16), rinv


def _layer_norm(x, g, b):
    mu = jnp.mean(x, axis=-1, keepdims=True)
    xc = x - mu
    var = jnp.mean(xc * xc, axis=-1, keepdims=True)
    return xc * lax.rsqrt(var + EPS) * g + b


def _sublane_groups(a):
    rows, lanes = a.shape
    return a.reshape(rows // SUBLANES, SUBLANES, lanes)


def _rows_later(a, d):
    a3 = _sublane_groups(a)
    rolled = pltpu.roll(a3, SUBLANES - d, axis=1)
    sub = lax.broadcasted_iota(jnp.int32, (1, SUBLANES, 1), 1)
    out = jnp.where(sub < SUBLANES - d, rolled[:-1], rolled[1:])
    return out.reshape(a.shape[0] - SUBLANES, a.shape[1])


def _rows_earlier(a, d):
    a3 = _sublane_groups(a)
    rolled = pltpu.roll(a3, d, axis=1)
    sub = lax.broadcasted_iota(jnp.int32, (1, SUBLANES, 1), 1)
    out = jnp.where(sub >= d, rolled[1:], rolled[:-1])
    return out.reshape(a.shape[0] - SUBLANES, a.shape[1])


def _gelu(x):
    return 0.5 * x * (1.0 + lax.erf(x * (2.0 ** -0.5)))


def _const_spec(shape):
    nd = len(shape)
    return pl.BlockSpec(shape, lambda *_: (0,) * nd, pipeline_mode=pl.Buffered(1))


def _tile_spec(tm):
    return pl.BlockSpec((tm, D_MODEL), lambda i: (i, 0))


def _cast_plan(w_stack, layer, n_chunks=N_CAST_CHUNKS):
    _, r, c = w_stack.shape
    assert r % (n_chunks * BF16_SUBLANES) == 0
    rows = r // n_chunks
    chunk = lambda i: jnp.minimum(i, n_chunks - 1)
    src = pl.BlockSpec((None, rows, c), lambda i: (layer, chunk(i), 0))
    dst = pl.BlockSpec((rows, c), lambda i: (chunk(i), 0))
    return src, dst, jax.ShapeDtypeStruct((r, c), BF16)


def _cast_step(step, srcs, dsts):
    @pl.when(step < N_CAST_CHUNKS)
    def _():
        for s, d in zip(srcs, dsts):
            d[...] = s[...].astype(BF16)


def _cast_kernel(s_ref, d_ref):
    d_ref[...] = s_ref[...].astype(BF16)


def _cast_bf16(w_stack, layer):
    src, dst, shape = _cast_plan(w_stack, layer, N_CAST_CHUNKS_ALONE)
    return pl.pallas_call(
        _cast_kernel, grid=(N_CAST_CHUNKS_ALONE,), in_specs=[src], out_specs=dst, out_shape=shape,
        compiler_params=pltpu.CompilerParams(
            dimension_semantics=("arbitrary",), vmem_limit_bytes=VMEM_LIMIT_BYTES),
        name="cast_bf16")(w_stack)


def _split_refs(refs, n_in, n_cast):
    bounds = [0, n_in, n_in + n_cast, n_in + n_cast + 1, n_in + 2 * n_cast + 1, len(refs)]
    ins, srcs, (o_ref,), dsts, scratch = [refs[lo:hi] for lo, hi in zip(bounds[:-1], bounds[1:])]
    return ins, srcs, o_ref, dsts, scratch


def _call(kern, name, tm, in_specs, args, next_weights, scratch_shapes):
    n_tok = args[0].shape[0]
    plans = [_cast_plan(w, layer) for w, layer in next_weights]
    outs = pl.pallas_call(
        kern,
        grid=(n_tok // tm,),
        in_specs=in_specs + [p[0] for p in plans],
        out_specs=[_tile_spec(tm)] + [p[1] for p in plans],
        out_shape=[jax.ShapeDtypeStruct((n_tok, D_MODEL), F32)] + [p[2] for p in plans],
        scratch_shapes=scratch_shapes,
        compiler_params=pltpu.CompilerParams(
            dimension_semantics=("arbitrary",), vmem_limit_bytes=VMEM_LIMIT_BYTES),
        name=name,
    )(*args, *[w for w, _ in next_weights])
    return outs[0], outs[1:]


def _make_ffn_kernel(final, n_cast):
    n_in = 5 if final else 4

    def kern(*refs):
        ins, srcs, o_ref, dsts, (a_ref,) = _split_refs(refs, n_in, n_cast)
        x_ref, g_ref, win_ref, wout_ref = ins[:4]
        _cast_step(pl.program_id(0), srcs, dsts)
        x = x_ref[...]
        xg, rinv = _norm_parts(x, g_ref[...])
        for c in range(N_FF_CHUNKS):
            cols = slice(c * FF_CHUNK, (c + 1) * FF_CHUNK)
            up_cols = slice(D_FF + c * FF_CHUNK, D_FF + (c + 1) * FF_CHUNK)
            gate = jnp.dot(xg, win_ref[:, cols], preferred_element_type=F32) * rinv
            up = jnp.dot(xg, win_ref[:, up_cols], preferred_element_type=F32) * rinv
            a_ref[:, cols] = (jax.nn.silu(gate) * up).astype(BF16)
        y = jnp.dot(a_ref[...], wout_ref[...], preferred_element_type=F32)
        out = x + 0.5 * y
        if final:
            out = _rms(out, ins[4][...])
        o_ref[...] = out

    return kern


def _ffn(x2, gain, win, wout, next_weights, final_gain=None):
    in_specs = [_tile_spec(TM_FFN), _const_spec((1, D_MODEL)), _const_spec(win.shape),
                _const_spec(wout.shape)]
    args = [x2, gain.reshape(1, D_MODEL), win, wout]
    final = final_gain is not None
    if final:
        in_specs.append(_const_spec((1, D_MODEL)))
        args.append(final_gain.reshape(1, D_MODEL))
    return _call(_make_ffn_kernel(final, len(next_weights)), "ffn_final" if final else "ffn",
                 TM_FFN, in_specs, args, next_weights, [pltpu.VMEM((TM_FFN, D_FF), BF16)])


def _make_pool_conv_kernel(tiles_per_seq, n_cast):
    def kern(*refs):
        ins, srcs, o_ref, dsts, scratch = _split_refs(refs, 11, n_cast)
        (x_ref, g_ref, win_ref, pw_ref, pb_ref, ps_ref, cw_ref, cb_ref,
         lg_ref, lb_ref, wout_ref) = ins
        ubuf, s2buf, s4buf, s8buf, gsh, ybuf = scratch
        step = pl.program_id(0)
        t = step % tiles_per_seq
        rows = HALO + TM_MIX
        _cast_step(step, srcs, dsts)

        @pl.when(t == 0)
        def _():
            ubuf[0:HALO, :] = jnp.zeros((HALO, POOL_CH), F32)
            gsh[0, 0:HALO, :] = jnp.zeros((HALO, CONV_CH), F32)

        x = x_ref[...]
        xn = _rms(x, g_ref[...]).astype(BF16)
        h = jnp.dot(xn, win_ref[...], preferred_element_type=F32)
        u = h[:, :POOL_CH]
        ubuf[HALO:, :] = u
        gsh[0, HALO:, :] = h[:, POOL_CH:POOL_CH + CONV_CH] * jax.nn.sigmoid(h[:, POOL_CH + CONV_CH:])

        g1 = POOL_GC
        s2buf[8:, :] = ubuf[8:, :] + _rows_earlier(ubuf[...], 1)
        s4buf[16:, :] = s2buf[16:, g1:] + _rows_earlier(s2buf[8:, g1:], 2)
        s8buf[24:, :] = s4buf[24:, g1:] + _rows_earlier(s4buf[16:, g1:], 4)
        s16 = s8buf[HALO:, g1:] + s8buf[pl.ds(HALO - 8, TM_MIX), g1:]
        sums = [s2buf[HALO:, :g1], s4buf[HALO:, :g1], s8buf[HALO:, :g1], s16]
        pos = t * TM_MIX + lax.broadcasted_iota(jnp.int32, (TM_MIX, 1), 0)
        pooled = []
        for gi, win in enumerate(POOL_WINDOWS):
            count = jnp.minimum(pos + 1, win).astype(F32)
            pooled.append(sums[gi] / count - u[:, gi * POOL_GC:(gi + 1) * POOL_GC])
        ya = []
        for p in range(2):
            pp = jnp.concatenate(pooled[2 * p:2 * p + 2], axis=-1).astype(BF16)
            ya.append(jnp.dot(pp, pw_ref[p], preferred_element_type=F32))
        ya = (jnp.concatenate(ya, axis=-1) + pb_ref[...]) * ps_ref[...]

        g_all = gsh[0]
        for r in range(1, 8):
            gsh[r, 0:CONV_SH_ROWS, :] = _rows_later(g_all, r)[0:CONV_SH_ROWS]

        taps_by_shift = [[] for _ in range(8)]
        for k in range(CONV_WIDTH):
            q, r = divmod(HALO - (CONV_WIDTH - 1) + k, 8)
            taps_by_shift[r].append((k, q))

        def conv_rows(rb, carry):
            base = pl.multiple_of(rb * CONV_RB, CONV_RB)
            for lb in range(CONV_CH // 128):
                lanes = slice(lb * 128, (lb + 1) * 128)
                acc = None
                for r, taps in enumerate(taps_by_shift):
                    span = CONV_RB + 8 * max(q for _, q in taps)
                    slab = gsh[r, pl.ds(base, span), lanes]
                    for k, q in taps:
                        term = slab[8 * q:8 * q + CONV_RB] * cw_ref[k:k + 1, lanes]
                        acc = term if acc is None else acc + term
                ybuf[pl.ds(base, CONV_RB), lanes] = acc
            return carry

        lax.fori_loop(0, TM_MIX // CONV_RB, conv_rows, 0)
        y = ybuf[...] + cb_ref[...]
        yb = jax.nn.silu(_layer_norm(y, lg_ref[...], lb_ref[...]))

        cat = jnp.concatenate([ya.astype(BF16), yb.astype(BF16)], axis=-1)
        o_ref[...] = x + jnp.dot(cat, wout_ref[...], preferred_element_type=F32)

        ubuf[0:HALO, :] = ubuf[TM_MIX:, :]
        gsh[0, 0:HALO, :] = gsh[0, TM_MIX:, :]

    return kern


def _pool_conv_mixer(x2, t_len, gain, win, wout, pool_w, pool_b, pool_scale, conv_w, conv_b,
                     ln_g, ln_b, next_weights):
    rows = HALO + TM_MIX
    pw = pool_w.astype(BF16)
    z = jnp.zeros((POOL_GC, POOL_GC), BF16)
    pw_bd = jnp.stack([
        jnp.block([[pw[0], z], [z, pw[1]]]),
        jnp.block([[pw[2], z], [z, pw[3]]]),
    ])
    row = lambda n: _const_spec((1, n))
    in_specs = [_tile_spec(TM_MIX), row(D_MODEL), _const_spec(win.shape), _const_spec(pw_bd.shape),
                row(POOL_CH), row(POOL_CH), _const_spec((CONV_WIDTH, CONV_CH)), row(CONV_CH),
                row(CONV_CH), row(CONV_CH), _const_spec(wout.shape)]
    args = [x2, gain.reshape(1, D_MODEL), win, pw_bd, pool_b.reshape(1, POOL_CH),
            pool_scale.reshape(1, POOL_CH), conv_w, conv_b.reshape(1, CONV_CH),
            ln_g.reshape(1, CONV_CH), ln_b.reshape(1, CONV_CH), wout]
    scratch = [pltpu.VMEM((rows, POOL_CH), F32),
               pltpu.VMEM((rows, POOL_CH), F32),
               pltpu.VMEM((rows, POOL_CH - POOL_GC), F32),
               pltpu.VMEM((rows, POOL_CH - 2 * POOL_GC), F32),
               pltpu.VMEM((8, rows, CONV_CH), F32),
               pltpu.VMEM((TM_MIX, CONV_CH), F32)]
    return _call(_make_pool_conv_kernel(t_len // TM_MIX, len(next_weights)), "pool_conv_mixer",
                 TM_MIX, in_specs, args, next_weights, scratch)


def _make_sgu_kernel(n_cast):
    def kern(*refs):
        ins, srcs, o_ref, dsts, (zv_ref, u_ref, v_ref, gated_ref) = _split_refs(refs, 8, n_cast)
        x_ref, g_ref, win_ref, lg_ref, lb_ref, ws_ref, bs_ref, wout_ref = ins
        n_chunks = TM_SGU // CHUNK
        n_cols = SGU_CH // MXU_TILE
        _cast_step(pl.program_id(0), srcs, dsts)

        x = x_ref[...]
        xn = _rms(x, g_ref[...]).astype(BF16)
        for c in range(n_cols):
            cols = slice(c * MXU_TILE, (c + 1) * MXU_TILE)
            w_cols = slice(SGU_CH + c * MXU_TILE, SGU_CH + (c + 1) * MXU_TILE)
            zv_ref[:, cols] = _gelu(jnp.dot(xn, win_ref[:, w_cols], preferred_element_type=F32))
        for c in range(n_cols):
            cols = slice(c * MXU_TILE, (c + 1) * MXU_TILE)
            u_ref[:, cols] = _gelu(jnp.dot(xn, win_ref[:, cols], preferred_element_type=F32))
        v_ref[...] = _layer_norm(zv_ref[...], lg_ref[...], lb_ref[...]).astype(BF16)

        row = lax.broadcasted_iota(jnp.int32, (CHUNK, CHUNK), 0)
        col = lax.broadcasted_iota(jnp.int32, (CHUNK, CHUNK), 1)
        mask = (col <= row).astype(F32)
        for hd in range(SGU_HEADS):
            lanes = slice(hd * SGU_HC, (hd + 1) * SGU_HC)
            w = (ws_ref[hd] * mask).astype(BF16)
            rhs = jnp.concatenate(
                [v_ref[n * CHUNK:(n + 1) * CHUNK, lanes] for n in range(n_chunks)], axis=-1)
            r = jnp.dot(w, rhs, preferred_element_type=F32)
            for n in range(n_chunks):
                rows_n = slice(n * CHUNK, (n + 1) * CHUNK)
                vo = r[:, n * SGU_HC:(n + 1) * SGU_HC] + bs_ref[:, lanes]
                gated_ref[rows_n, lanes] = (u_ref[rows_n, lanes] * vo).astype(BF16)
        o_ref[...] = x + jnp.dot(gated_ref[...], wout_ref[...], preferred_element_type=F32)

    return kern


def _sgu_mixer(x2, gain, win, wout, ln_g, ln_b, w_s, b_s, next_weights):
    bias = jnp.repeat(b_s.T, SGU_HC, axis=1)
    in_specs = [_tile_spec(TM_SGU), _const_spec((1, D_MODEL)), _const_spec(win.shape),
                _const_spec((1, SGU_CH)), _const_spec((1, SGU_CH)),
                _const_spec((SGU_HEADS, CHUNK, CHUNK)), _const_spec((CHUNK, SGU_CH)),
                _const_spec(wout.shape)]
    args = [x2, gain.reshape(1, D_MODEL), win, ln_g.reshape(1, SGU_CH), ln_b.reshape(1, SGU_CH),
            w_s, bias, wout]
    scratch = [pltpu.VMEM((TM_SGU, SGU_CH), F32),
               pltpu.VMEM((TM_SGU, SGU_CH), F32),
               pltpu.VMEM((TM_SGU, SGU_CH), BF16),
               pltpu.VMEM((TM_SGU, SGU_CH), BF16)]
    return _call(_make_sgu_kernel(len(next_weights)), "sgu_mixer", TM_SGU, in_specs, args,
                 next_weights, scratch)


def kernel(x, ffn1_norm, ffn1_w_in, ffn1_w_out, mix_norm, ffn2_norm, ffn2_w_in, ffn2_w_out,
           ab_w_in, pool_w, pool_b, pool_scale, conv_w, conv_b, conv_ln_g, conv_ln_b, ab_w_out,
           sgu_w_in, sgu_ln_g, sgu_ln_b, sgu_w, sgu_b, sgu_w_out, final_norm):
    bsz, t_len, d = x.shape
    depth = ffn1_norm.shape[0]
    assert d == D_MODEL and t_len % max(TM_MIX, TM_SGU, TM_FFN) == 0
    assert (bsz * t_len) // max(TM_MIX, TM_SGU, TM_FFN) >= N_CAST_CHUNKS
    x2 = x.reshape(bsz * t_len, d)
    w = [_cast_bf16(ffn1_w_in, 0), _cast_bf16(ffn1_w_out, 0)]
    for i in range(depth):
        j = i // 2
        pool_layer = i % 2 == 0
        mixer_weights = [(ab_w_in, j), (ab_w_out, j)] if pool_layer else [(sgu_w_in, j), (sgu_w_out, j)]
        x2, w = _ffn(x2, ffn1_norm[i], w[0], w[1], mixer_weights)
        ffn2_weights = [(ffn2_w_in, i), (ffn2_w_out, i)]
        if pool_layer:
            x2, w = _pool_conv_mixer(x2, t_len, mix_norm[i], w[0], w[1], pool_w[j], pool_b[j],
                                     pool_scale[j], conv_w[j], conv_b[j], conv_ln_g[j], conv_ln_b[j],
                                     ffn2_weights)
        else:
            x2, w = _sgu_mixer(x2, mix_norm[i], w[0], w[1], sgu_ln_g[j], sgu_ln_b[j], sgu_w[j],
                               sgu_b[j], ffn2_weights)
        last = i == depth - 1
        next_ffn1 = [] if last else [(ffn1_w_in, i + 1), (ffn1_w_out, i + 1)]
        x2, w = _ffn(x2, ffn2_norm[i], w[0], w[1], next_ffn1, final_norm if last else None)
    return x2.reshape(bsz, t_len, d)
```

```python
import jax
import jax.numpy as jnp
from jax import lax
from jax.experimental import pallas as pl
from jax.experimental.pallas import tpu as pltpu

D_MODEL = 1024
D_FF = 2816
POOL_CH = 512
POOL_WINDOWS = (2, 4, 8, 16)
POOL_GC = 128
CONV_CH = 512
CONV_WIDTH = 31
AB_IN = POOL_CH + 2 * CONV_CH
SGU_CH = 1024
SGU_HEADS = 8
SGU_HC = 128
CHUNK = 128
EPS = 1e-6

MXU_TILE = 256
FF_CHUNK = MXU_TILE
N_FF_CHUNKS = D_FF // FF_CHUNK

TM_FFN = 1024
TM_MIX = 1024
TM_SGU = 1024

HALO = 32
CONV_RB = 64
CONV_SH_ROWS = TM_MIX + 24

N_CAST_CHUNKS = 16
N_CAST_CHUNKS_ALONE = 4
SUBLANES = 8
BF16_SUBLANES = 16

VMEM_LIMIT_BYTES = 56 * 1024 * 1024

F32 = jnp.float32
BF16 = jnp.bfloat16


def _rms(x, g):
    return x * lax.rsqrt(jnp.mean(x * x, axis=-1, keepdims=True) + EPS) * g


def _norm_parts(x, g):
    rinv = lax.rsqrt(jnp.mean(x * x, axis=-1, keepdims=True) + EPS)
    return (x * g).astype(BF16), rinv


def _layer_norm(x, g, b):
    mu = jnp.mean(x, axis=-1, keepdims=True)
    xc = x - mu
    var = jnp.mean(xc * xc, axis=-1, keepdims=True)
    return xc * lax.rsqrt(var + EPS) * g + b


def _sublane_groups(a):
    rows, lanes = a.shape
    return a.reshape(rows // SUBLANES, SUBLANES, lanes)


def _rows_later(a, d):
    a3 = _sublane_groups(a)
    rolled = pltpu.roll(a3, SUBLANES - d, axis=1)
    sub = lax.broadcasted_iota(jnp.int32, (1, SUBLANES, 1), 1)
    out = jnp.where(sub < SUBLANES - d, rolled[:-1], rolled[1:])
    return out.reshape(a.shape[0] - SUBLANES, a.shape[1])


def _rows_earlier(a, d):
    a3 = _sublane_groups(a)
    rolled = pltpu.roll(a3, d, axis=1)
    sub = lax.broadcasted_iota(jnp.int32, (1, SUBLANES, 1), 1)
    out = jnp.where(sub >= d, rolled[1:], rolled[:-1])
    return out.reshape(a.shape[0] - SUBLANES, a.shape[1])


def _gelu(x):
    return 0.5 * x * (1.0 + lax.erf(x * (2.0 ** -0.5)))


def _const_spec(shape):
    nd = len(shape)
    return pl.BlockSpec(shape, lambda *_: (0,) * nd, pipeline_mode=pl.Buffered(1))


def _tile_spec(tm):
    return pl.BlockSpec((tm, D_MODEL), lambda i: (i, 0))


def _cast_plan(w_stack, layer, n_chunks=N_CAST_CHUNKS):
    _, r, c = w_stack.shape
    assert r % (n_chunks * BF16_SUBLANES) == 0
    rows = r // n_chunks
    chunk = lambda i: jnp.minimum(i, n_chunks - 1)
    src = pl.BlockSpec((None, rows, c), lambda i: (layer, chunk(i), 0))
    dst = pl.BlockSpec((rows, c), lambda i: (chunk(i), 0))
    return src, dst, jax.ShapeDtypeStruct((r, c), BF16)


def _cast_step(step, srcs, dsts):
    @pl.when(step < N_CAST_CHUNKS)
    def _():
        for s, d in zip(srcs, dsts):
            d[...] = s[...].astype(BF16)


def _cast_kernel(s_ref, d_ref):
    d_ref[...] = s_ref[...].astype(BF16)


def _cast_bf16(w_stack, layer):
    src, dst, shape = _cast_plan(w_stack, layer, N_CAST_CHUNKS_ALONE)
    return pl.pallas_call(
        _cast_kernel, grid=(N_CAST_CHUNKS_ALONE,), in_specs=[src], out_specs=dst, out_shape=shape,
        compiler_params=pltpu.CompilerParams(
            dimension_semantics=("arbitrary",), vmem_limit_bytes=VMEM_LIMIT_BYTES),
        name="cast_bf16")(w_stack)


def _split_refs(refs, n_in, n_cast):
    bounds = [0, n_in, n_in + n_cast, n_in + n_cast + 1, n_in + 2 * n_cast + 1, len(refs)]
    ins, srcs, (o_ref,), dsts, scratch = [refs[lo:hi] for lo, hi in zip(bounds[:-1], bounds[1:])]
    return ins, srcs, o_ref, dsts, scratch


def _call(kern, name, tm, in_specs, args, next_weights, scratch_shapes):
    n_tok = args[0].shape[0]
    plans = [_cast_plan(w, layer) for w, layer in next_weights]
    outs = pl.pallas_call(
        kern,
        grid=(n_tok // tm,),
        in_specs=in_specs + [p[0] for p in plans],
        out_specs=[_tile_spec(tm)] + [p[1] for p in plans],
        out_shape=[jax.ShapeDtypeStruct((n_tok, D_MODEL), F32)] + [p[2] for p in plans],
        scratch_shapes=scratch_shapes,
        compiler_params=pltpu.CompilerParams(
            dimension_semantics=("arbitrary",), vmem_limit_bytes=VMEM_LIMIT_BYTES),
        name=name,
    )(*args, *[w for w, _ in next_weights])
    return outs[0], outs[1:]


def _make_ffn_kernel(final, n_cast):
    n_in = 5 if final else 4

    def kern(*refs):
        ins, srcs, o_ref, dsts, (a_ref,) = _split_refs(refs, n_in, n_cast)
        x_ref, g_ref, win_ref, wout_ref = ins[:4]
        _cast_step(pl.program_id(0), srcs, dsts)
        x = x_ref[...]
        xg, rinv = _norm_parts(x, g_ref[...])
        for c in range(N_FF_CHUNKS):
            cols = slice(c * FF_CHUNK, (c + 1) * FF_CHUNK)
            up_cols = slice(D_FF + c * FF_CHUNK, D_FF + (c + 1) * FF_CHUNK)
            gate = jnp.dot(xg, win_ref[:, cols], preferred_element_type=F32) * rinv
            up = jnp.dot(xg, win_ref[:, up_cols], preferred_element_type=F32) * rinv
            a_ref[:, cols] = (jax.nn.silu(gate) * up).astype(BF16)
        y = jnp.dot(a_ref[...], wout_ref[...], preferred_element_type=F32)
        out = x + 0.5 * y
        if final:
            out = _rms(out, ins[4][...])
        o_ref[...] = out

    return kern


def _ffn(x2, gain, win, wout, next_weights, final_gain=None):
    in_specs = [_tile_spec(TM_FFN), _const_spec((1, D_MODEL)), _const_spec(win.shape),
                _const_spec(wout.shape)]
    args = [x2, gain.reshape(1, D_MODEL), win, wout]
    final = final_gain is not None
    if final:
        in_specs.append(_const_spec((1, D_MODEL)))
        args.append(final_gain.reshape(1, D_MODEL))
    return _call(_make_ffn_kernel(final, len(next_weights)), "ffn_final" if final else "ffn",
                 TM_FFN, in_specs, args, next_weights, [pltpu.VMEM((TM_FFN, D_FF), BF16)])


def _make_pool_conv_kernel(tiles_per_seq, n_cast):
    def kern(*refs):
        ins, srcs, o_ref, dsts, scratch = _split_refs(refs, 11, n_cast)
        (x_ref, g_ref, win_ref, pw_ref, pb_ref, ps_ref, cw_ref, cb_ref,
         lg_ref, lb_ref, wout_ref) = ins
        ubuf, s2buf, s4buf, s8buf, gsh, ybuf = scratch
        step = pl.program_id(0)
        t = step % tiles_per_seq
        rows = HALO + TM_MIX
        _cast_step(step, srcs, dsts)

        @pl.when(t == 0)
        def _():
            ubuf[0:HALO, :] = jnp.zeros((HALO, POOL_CH), F32)
            gsh[0, 0:HALO, :] = jnp.zeros((HALO, CONV_CH), F32)

        x = x_ref[...]
        xn = _rms(x, g_ref[...]).astype(BF16)
        h = jnp.dot(xn, win_ref[...], preferred_element_type=F32)
        u = h[:, :POOL_CH]
        ubuf[HALO:, :] = u
        gsh[0, HALO:, :] = h[:, POOL_CH:POOL_CH + CONV_CH] * jax.nn.sigmoid(h[:, POOL_CH + CONV_CH:])

        g1 = POOL_GC
        s2buf[8:, :] = ubuf[8:, :] + _rows_earlier(ubuf[...], 1)
        s4buf[16:, :] = s2buf[16:, g1:] + _rows_earlier(s2buf[8:, g1:], 2)
        s8buf[24:, :] = s4buf[24:, g1:] + _rows_earlier(s4buf[16:, g1:], 4)
        s16 = s8buf[HALO:, g1:] + s8buf[pl.ds(HALO - 8, TM_MIX), g1:]
        sums = [s2buf[HALO:, :g1], s4buf[HALO:, :g1], s8buf[HALO:, :g1], s16]
        pos = t * TM_MIX + lax.broadcasted_iota(jnp.int32, (TM_MIX, 1), 0)
        pooled = []
        for gi, win in enumerate(POOL_WINDOWS):
            count = jnp.minimum(pos + 1, win).astype(F32)
            pooled.append(sums[gi] / count - u[:, gi * POOL_GC:(gi + 1) * POOL_GC])
        ya = []
        for p in range(2):
            pp = jnp.concatenate(pooled[2 * p:2 * p + 2], axis=-1).astype(BF16)
            ya.append(jnp.dot(pp, pw_ref[p], preferred_element_type=F32))
        ya = (jnp.concatenate(ya, axis=-1) + pb_ref[...]) * ps_ref[...]

        g_all = gsh[0]
        for r in range(1, 8):
            gsh[r, 0:CONV_SH_ROWS, :] = _rows_later(g_all, r)[0:CONV_SH_ROWS]

        taps_by_shift = [[] for _ in range(8)]
        for k in range(CONV_WIDTH):
            q, r = divmod(HALO - (CONV_WIDTH - 1) + k, 8)
            taps_by_shift[r].append((k, q))

        def conv_rows(rb, carry):
            base = pl.multiple_of(rb * CONV_RB, CONV_RB)
            for lb in range(CONV_CH // 128):
                lanes = slice(lb * 128, (lb + 1) * 128)
                acc = None
                for r, taps in enumerate(taps_by_shift):
                    span = CONV_RB + 8 * max(q for _, q in taps)
                    slab = gsh[r, pl.ds(base, span), lanes]
                    for k, q in taps:
                        term = slab[8 * q:8 * q + CONV_RB] * cw_ref[k:k + 1, lanes]
                        acc = term if acc is None else acc + term
                ybuf[pl.ds(base, CONV_RB), lanes] = acc
            return carry

        lax.fori_loop(0, TM_MIX // CONV_RB, conv_rows, 0)
        y = ybuf[...] + cb_ref[...]
        yb = jax.nn.silu(_layer_norm(y, lg_ref[...], lb_ref[...]))

        cat = jnp.concatenate([ya.astype(BF16), yb.astype(BF16)], axis=-1)
        o_ref[...] = x + jnp.dot(cat, wout_ref[...], preferred_element_type=F32)

        ubuf[0:HALO, :] = ubuf[TM_MIX:, :]
        gsh[0, 0:HALO, :] = gsh[0, TM_MIX:, :]

    return kern


def _pool_conv_mixer(x2, t_len, gain, win, wout, pool_w, pool_b, pool_scale, conv_w, conv_b,
                     ln_g, ln_b, next_weights):
    rows = HALO + TM_MIX
    pw = pool_w.astype(BF16)
    z = jnp.zeros((POOL_GC, POOL_GC), BF16)
    pw_bd = jnp.stack([
        jnp.block([[pw[0], z], [z, pw[1]]]),
        jnp.block([[pw[2], z], [z, pw[3]]]),
    ])
    row = lambda n: _const_spec((1, n))
    in_specs = [_tile_spec(TM_MIX), row(D_MODEL), _const_spec(win.shape), _const_spec(pw_bd.shape),
                row(POOL_CH), row(POOL_CH), _const_spec((CONV_WIDTH, CONV_CH)), row(CONV_CH),
                row(CONV_CH), row(CONV_CH), _const_spec(wout.shape)]
    args = [x2, gain.reshape(1, D_MODEL), win, pw_bd, pool_b.reshape(1, POOL_CH),
            pool_scale.reshape(1, POOL_CH), conv_w, conv_b.reshape(1, CONV_CH),
            ln_g.reshape(1, CONV_CH), ln_b.reshape(1, CONV_CH), wout]
    scratch = [pltpu.VMEM((rows, POOL_CH), F32),
               pltpu.VMEM((rows, POOL_CH), F32),
               pltpu.VMEM((rows, POOL_CH - POOL_GC), F32),
               pltpu.VMEM((rows, POOL_CH - 2 * POOL_GC), F32),
               pltpu.VMEM((8, rows, CONV_CH), F32),
               pltpu.VMEM((TM_MIX, CONV_CH), F32)]
    return _call(_make_pool_conv_kernel(t_len // TM_MIX, len(next_weights)), "pool_conv_mixer",
                 TM_MIX, in_specs, args, next_weights, scratch)


def _make_sgu_kernel(n_cast):
    def kern(*refs):
        ins, srcs, o_ref, dsts, (zv_ref, u_ref, v_ref, gated_ref) = _split_refs(refs, 8, n_cast)
        x_ref, g_ref, win_ref, lg_ref, lb_ref, ws_ref, bs_ref, wout_ref = ins
        n_chunks = TM_SGU // CHUNK
        n_cols = SGU_CH // MXU_TILE
        _cast_step(pl.program_id(0), srcs, dsts)

        x = x_ref[...]
        xn = _rms(x, g_ref[...]).astype(BF16)
        for c in range(n_cols):
            cols = slice(c * MXU_TILE, (c + 1) * MXU_TILE)
            w_cols = slice(SGU_CH + c * MXU_TILE, SGU_CH + (c + 1) * MXU_TILE)
            zv_ref[:, cols] = _gelu(jnp.dot(xn, win_ref[:, w_cols], preferred_element_type=F32))
        heads_per_tile = MXU_TILE // SGU_HC
        for c in range(n_cols):
            cols = slice(c * MXU_TILE, (c + 1) * MXU_TILE)
            uc = _gelu(jnp.dot(xn, win_ref[:, cols], preferred_element_type=F32))
            for k in range(heads_per_tile):
                u_ref[c * heads_per_tile + k] = uc[:, k * SGU_HC:(k + 1) * SGU_HC]
        v = _layer_norm(zv_ref[...], lg_ref[...], lb_ref[...]).astype(BF16)
        for hd in range(SGU_HEADS):
            v_ref[hd] = v[:, hd * SGU_HC:(hd + 1) * SGU_HC]

        row = lax.broadcasted_iota(jnp.int32, (CHUNK, CHUNK), 0)
        col = lax.broadcasted_iota(jnp.int32, (CHUNK, CHUNK), 1)
        mask = (col <= row).astype(F32)
        for hd in range(SGU_HEADS):
            lanes = slice(hd * SGU_HC, (hd + 1) * SGU_HC)
            w = (ws_ref[hd] * mask).astype(BF16)
            rhs = jnp.concatenate(
                [v_ref[hd, n * CHUNK:(n + 1) * CHUNK, :] for n in range(n_chunks)], axis=-1)
            r = jnp.dot(w, rhs, preferred_element_type=F32)
            for n in range(n_chunks):
                rows_n = slice(n * CHUNK, (n + 1) * CHUNK)
                vo = r[:, n * SGU_HC:(n + 1) * SGU_HC] + bs_ref[:, lanes]
                gated_ref[rows_n, lanes] = (u_ref[hd, rows_n, :] * vo).astype(BF16)
        o_ref[...] = x + jnp.dot(gated_ref[...], wout_ref[...], preferred_element_type=F32)

    return kern


def _sgu_mixer(x2, gain, win, wout, ln_g, ln_b, w_s, b_s, next_weights):
    bias = jnp.repeat(b_s.T, SGU_HC, axis=1)
    in_specs = [_tile_spec(TM_SGU), _const_spec((1, D_MODEL)), _const_spec(win.shape),
                _const_spec((1, SGU_CH)), _const_spec((1, SGU_CH)),
                _const_spec((SGU_HEADS, CHUNK, CHUNK)), _const_spec((CHUNK, SGU_CH)),
                _const_spec(wout.shape)]
    args = [x2, gain.reshape(1, D_MODEL), win, ln_g.reshape(1, SGU_CH), ln_b.reshape(1, SGU_CH),
            w_s, bias, wout]
    scratch = [pltpu.VMEM((TM_SGU, SGU_CH), F32),
               pltpu.VMEM((SGU_HEADS, TM_SGU, SGU_HC), F32),
               pltpu.VMEM((SGU_HEADS, TM_SGU, SGU_HC), BF16),
               pltpu.VMEM((TM_SGU, SGU_CH), BF16)]
    return _call(_make_sgu_kernel(len(next_weights)), "sgu_mixer", TM_SGU, in_specs, args,
                 next_weights, scratch)


def kernel(x, ffn1_norm, ffn1_w_in, ffn1_w_out, mix_norm, ffn2_norm, ffn2_w_in, ffn2_w_out,
           ab_w_in, pool_w, pool_b, pool_scale, conv_w, conv_b, conv_ln_g, conv_ln_b, ab_w_out,
           sgu_w_in, sgu_ln_g, sgu_ln_b, sgu_w, sgu_b, sgu_w_out, final_norm):
    bsz, t_len, d = x.shape
    depth = ffn1_norm.shape[0]
    assert d == D_MODEL and t_len % max(TM_MIX, TM_SGU, TM_FFN) == 0
    assert (bsz * t_len) // max(TM_MIX, TM_SGU, TM_FFN) >= N_CAST_CHUNKS
    x2 = x.reshape(bsz * t_len, d)
    w = [_cast_bf16(ffn1_w_in, 0), _cast_bf16(ffn1_w_out, 0)]
    for i in range(depth):
        j = i // 2
        pool_layer = i % 2 == 0
        mixer_weights = [(ab_w_in, j), (ab_w_out, j)] if pool_layer else [(sgu_w_in, j), (sgu_w_out, j)]
        x2, w = _ffn(x2, ffn1_norm[i], w[0], w[1], mixer_weights + [(ffn2_w_in, i), (ffn2_w_out, i)])
        w_mix, w_ffn2 = w[:2], w[2:]
        if pool_layer:
            x2, _ = _pool_conv_mixer(x2, t_len, mix_norm[i], w_mix[0], w_mix[1], pool_w[j], pool_b[j],
                                     pool_scale[j], conv_w[j], conv_b[j], conv_ln_g[j], conv_ln_b[j], [])
        else:
            x2, _ = _sgu_mixer(x2, mix_norm[i], w_mix[0], w_mix[1], sgu_ln_g[j], sgu_ln_b[j], sgu_w[j],
                               sgu_b[j], [])
        last = i == depth - 1
        next_ffn1 = [] if last else [(ffn1_w_in, i + 1), (ffn1_w_out, i + 1)]
        x2, w = _ffn(x2, ffn2_norm[i], w_ffn2[0], w_ffn2[1], next_ffn1, final_norm if last else None)
    return x2.reshape(bsz, t_len, d)
```

```python
import jax
import jax.numpy as jnp
from jax import lax
from jax.experimental import pallas as pl
from jax.experimental.pallas import tpu as pltpu

D_MODEL = 1024
D_FF = 2816
POOL_CH = 512
POOL_WINDOWS = (2, 4, 8, 16)
POOL_GC = 128
CONV_CH = 512
CONV_WIDTH = 31
AB_IN = POOL_CH + 2 * CONV_CH
SGU_CH = 1024
SGU_HEADS = 8
SGU_HC = 128
CHUNK = 128
EPS = 1e-6

MXU_TILE = 256
FF_CHUNK = MXU_TILE
N_FF_CHUNKS = D_FF // FF_CHUNK

TM_FFN = 1024
TM_MIX = 1024
TM_SGU = 1024

HALO = 32
CONV_RB = 64
CONV_SH_ROWS = TM_MIX + 24

N_CAST_CHUNKS = 16
N_CAST_CHUNKS_ALONE = 4
SUBLANES = 8
BF16_SUBLANES = 16

VMEM_LIMIT_BYTES = 56 * 1024 * 1024

F32 = jnp.float32
BF16 = jnp.bfloat16


def _rms(x, g):
    return x * lax.rsqrt(jnp.mean(x * x, axis=-1, keepdims=True) + EPS) * g


def _norm_parts(x, g):
    rinv = lax.rsqrt(jnp.mean(x * x, axis=-1, keepdims=True) + EPS)
    return (x * g).astype(BF16), rinv


def _layer_norm(x, g, b):
    mu = jnp.mean(x, axis=-1, keepdims=True)
    xc = x - mu
    var = jnp.mean(xc * xc, axis=-1, keepdims=True)
    return xc * lax.rsqrt(var + EPS) * g + b


def _sublane_groups(a):
    rows, lanes = a.shape
    return a.reshape(rows // SUBLANES, SUBLANES, lanes)


def _rows_later(a, d):
    a3 = _sublane_groups(a)
    rolled = pltpu.roll(a3, SUBLANES - d, axis=1)
    sub = lax.broadcasted_iota(jnp.int32, (1, SUBLANES, 1), 1)
    out = jnp.where(sub < SUBLANES - d, rolled[:-1], rolled[1:])
    return out.reshape(a.shape[0] - SUBLANES, a.shape[1])


def _rows_earlier(a, d):
    a3 = _sublane_groups(a)
    rolled = pltpu.roll(a3, d, axis=1)
    sub = lax.broadcasted_iota(jnp.int32, (1, SUBLANES, 1), 1)
    out = jnp.where(sub >= d, rolled[1:], rolled[:-1])
    return out.reshape(a.shape[0] - SUBLANES, a.shape[1])


def _gelu(x):
    return 0.5 * x * (1.0 + lax.erf(x * (2.0 ** -0.5)))


def _const_spec(shape):
    nd = len(shape)
    return pl.BlockSpec(shape, lambda *_: (0,) * nd, pipeline_mode=pl.Buffered(1))


def _tile_spec(tm):
    return pl.BlockSpec((tm, D_MODEL), lambda i: (i, 0))


def _cast_plan(w_stack, layer, n_chunks=N_CAST_CHUNKS):
    _, r, c = w_stack.shape
    assert r % (n_chunks * BF16_SUBLANES) == 0
    rows = r // n_chunks
    chunk = lambda i: jnp.minimum(i, n_chunks - 1)
    src = pl.BlockSpec((None, rows, c), lambda i: (layer, chunk(i), 0))
    dst = pl.BlockSpec((rows, c), lambda i: (chunk(i), 0))
    return src, dst, jax.ShapeDtypeStruct((r, c), BF16)


def _cast_step(step, srcs, dsts):
    @pl.when(step < N_CAST_CHUNKS)
    def _():
        for s, d in zip(srcs, dsts):
            d[...] = s[...].astype(BF16)


def _cast_kernel(s_ref, d_ref):
    d_ref[...] = s_ref[...].astype(BF16)


def _cast_bf16(w_stack, layer):
    src, dst, shape = _cast_plan(w_stack, layer, N_CAST_CHUNKS_ALONE)
    return pl.pallas_call(
        _cast_kernel, grid=(N_CAST_CHUNKS_ALONE,), in_specs=[src], out_specs=dst, out_shape=shape,
        compiler_params=pltpu.CompilerParams(
            dimension_semantics=("arbitrary",), vmem_limit_bytes=VMEM_LIMIT_BYTES),
        name="cast_bf16")(w_stack)


def _split_refs(refs, n_in, n_cast):
    bounds = [0, n_in, n_in + n_cast, n_in + n_cast + 1, n_in + 2 * n_cast + 1, len(refs)]
    ins, srcs, (o_ref,), dsts, scratch = [refs[lo:hi] for lo, hi in zip(bounds[:-1], bounds[1:])]
    return ins, srcs, o_ref, dsts, scratch


def _call(kern, name, tm, in_specs, args, next_weights, scratch_shapes):
    n_tok = args[0].shape[0]
    plans = [_cast_plan(w, layer) for w, layer in next_weights]
    outs = pl.pallas_call(
        kern,
        grid=(n_tok // tm,),
        in_specs=in_specs + [p[0] for p in plans],
        out_specs=[_tile_spec(tm)] + [p[1] for p in plans],
        out_shape=[jax.ShapeDtypeStruct((n_tok, D_MODEL), F32)] + [p[2] for p in plans],
        scratch_shapes=scratch_shapes,
        compiler_params=pltpu.CompilerParams(
            dimension_semantics=("arbitrary",), vmem_limit_bytes=VMEM_LIMIT_BYTES),
        name=name,
    )(*args, *[w for w, _ in next_weights])
    return outs[0], outs[1:]


def _make_ffn_kernel(final, n_cast):
    n_in = 5 if final else 4

    def kern(*refs):
        ins, srcs, o_ref, dsts, (a_ref,) = _split_refs(refs, n_in, n_cast)
        x_ref, g_ref, win_ref, wout_ref = ins[:4]
        _cast_step(pl.program_id(0), srcs, dsts)
        x = x_ref[...]
        xg, rinv = _norm_parts(x, g_ref[...])
        for c in range(N_FF_CHUNKS):
            cols = slice(c * FF_CHUNK, (c + 1) * FF_CHUNK)
            up_cols = slice(D_FF + c * FF_CHUNK, D_FF + (c + 1) * FF_CHUNK)
            gate = jnp.dot(xg, win_ref[:, cols], preferred_element_type=F32) * rinv
            up = jnp.dot(xg, win_ref[:, up_cols], preferred_element_type=F32) * rinv
            a_ref[:, cols] = (jax.nn.silu(gate) * up).astype(BF16)
        y = jnp.dot(a_ref[...], wout_ref[...], preferred_element_type=F32)
        out = x + 0.5 * y
        if final:
            out = _rms(out, ins[4][...])
        o_ref[...] = out

    return kern


def _ffn(x2, gain, win, wout, next_weights, final_gain=None):
    in_specs = [_tile_spec(TM_FFN), _const_spec((1, D_MODEL)), _const_spec(win.shape),
                _const_spec(wout.shape)]
    args = [x2, gain.reshape(1, D_MODEL), win, wout]
    final = final_gain is not None
    if final:
        in_specs.append(_const_spec((1, D_MODEL)))
        args.append(final_gain.reshape(1, D_MODEL))
    return _call(_make_ffn_kernel(final, len(next_weights)), "ffn_final" if final else "ffn",
                 TM_FFN, in_specs, args, next_weights, [pltpu.VMEM((TM_FFN, D_FF), BF16)])


def _make_pool_conv_kernel(tiles_per_seq, n_cast):
    def kern(*refs):
        ins, srcs, o_ref, dsts, scratch = _split_refs(refs, 11, n_cast)
        (x_ref, g_ref, win_ref, pw_ref, pb_ref, ps_ref, cw_ref, cb_ref,
         lg_ref, lb_ref, wout_ref) = ins
        ubuf, s2buf, s4buf, s8buf, gsh, g2, ybuf = scratch
        step = pl.program_id(0)
        t = step % tiles_per_seq
        rows = HALO + TM_MIX
        _cast_step(step, srcs, dsts)

        @pl.when(t == 0)
        def _():
            ubuf[0:HALO, :] = jnp.zeros((HALO, POOL_CH), F32)
            gsh[0, 0:HALO, :] = jnp.zeros((HALO, CONV_CH), F32)

        x = x_ref[...]
        xn = _rms(x, g_ref[...]).astype(BF16)
        h = jnp.dot(xn, win_ref[...], preferred_element_type=F32)
        u = h[:, :POOL_CH]
        ubuf[HALO:, :] = u
        gsh[0, HALO:, :] = h[:, POOL_CH:POOL_CH + CONV_CH] * jax.nn.sigmoid(h[:, POOL_CH + CONV_CH:])

        g1 = POOL_GC
        s2buf[8:, :] = ubuf[8:, :] + _rows_earlier(ubuf[...], 1)
        s4buf[16:, :] = s2buf[16:, g1:] + _rows_earlier(s2buf[8:, g1:], 2)
        s8buf[24:, :] = s4buf[24:, g1:] + _rows_earlier(s4buf[16:, g1:], 4)
        s16 = s8buf[HALO:, g1:] + s8buf[pl.ds(HALO - 8, TM_MIX), g1:]
        sums = [s2buf[HALO:, :g1], s4buf[HALO:, :g1], s8buf[HALO:, :g1], s16]
        pos = t * TM_MIX + lax.broadcasted_iota(jnp.int32, (TM_MIX, 1), 0)
        pooled = []
        for gi, win in enumerate(POOL_WINDOWS):
            count = jnp.minimum(pos + 1, win).astype(F32)
            pooled.append(sums[gi] / count - u[:, gi * POOL_GC:(gi + 1) * POOL_GC])
        ya = []
        for p in range(2):
            pp = jnp.concatenate(pooled[2 * p:2 * p + 2], axis=-1).astype(BF16)
            ya.append(jnp.dot(pp, pw_ref[p], preferred_element_type=F32))
        ya = (jnp.concatenate(ya, axis=-1) + pb_ref[...]) * ps_ref[...]

        for lb in range(CONV_CH // 128):
            lanes = slice(lb * 128, (lb + 1) * 128)
            g2[lb, pl.ds(0, rows, stride=2), :] = gsh[0, :, lanes]
            for r in range(1, 8):
                gsh[r, 0:CONV_SH_ROWS, lanes] = g2[lb, pl.ds(2 * r, CONV_SH_ROWS, stride=2), :]

        taps_by_shift = [[] for _ in range(8)]
        for k in range(CONV_WIDTH):
            q, r = divmod(HALO - (CONV_WIDTH - 1) + k, 8)
            taps_by_shift[r].append((k, q))

        def conv_rows(rb, carry):
            base = pl.multiple_of(rb * CONV_RB, CONV_RB)
            for lb in range(CONV_CH // 128):
                lanes = slice(lb * 128, (lb + 1) * 128)
                acc = None
                for r, taps in enumerate(taps_by_shift):
                    span = CONV_RB + 8 * max(q for _, q in taps)
                    slab = gsh[r, pl.ds(base, span), lanes]
                    for k, q in taps:
                        term = slab[8 * q:8 * q + CONV_RB] * cw_ref[k:k + 1, lanes]
                        acc = term if acc is None else acc + term
                ybuf[pl.ds(base, CONV_RB), lanes] = acc
            return carry

        lax.fori_loop(0, TM_MIX // CONV_RB, conv_rows, 0)
        y = ybuf[...] + cb_ref[...]
        yb = jax.nn.silu(_layer_norm(y, lg_ref[...], lb_ref[...]))

        cat = jnp.concatenate([ya.astype(BF16), yb.astype(BF16)], axis=-1)
        o_ref[...] = x + jnp.dot(cat, wout_ref[...], preferred_element_type=F32)

        ubuf[0:HALO, :] = ubuf[TM_MIX:, :]
        gsh[0, 0:HALO, :] = gsh[0, TM_MIX:, :]

    return kern


def _pool_conv_mixer(x2, t_len, gain, win, wout, pool_w, pool_b, pool_scale, conv_w, conv_b,
                     ln_g, ln_b, next_weights):
    rows = HALO + TM_MIX
    pw = pool_w.astype(BF16)
    z = jnp.zeros((POOL_GC, POOL_GC), BF16)
    pw_bd = jnp.stack([
        jnp.block([[pw[0], z], [z, pw[1]]]),
        jnp.block([[pw[2], z], [z, pw[3]]]),
    ])
    row = lambda n: _const_spec((1, n))
    in_specs = [_tile_spec(TM_MIX), row(D_MODEL), _const_spec(win.shape), _const_spec(pw_bd.shape),
                row(POOL_CH), row(POOL_CH), _const_spec((CONV_WIDTH, CONV_CH)), row(CONV_CH),
                row(CONV_CH), row(CONV_CH), _const_spec(wout.shape)]
    args = [x2, gain.reshape(1, D_MODEL), win, pw_bd, pool_b.reshape(1, POOL_CH),
            pool_scale.reshape(1, POOL_CH), conv_w, conv_b.reshape(1, CONV_CH),
            ln_g.reshape(1, CONV_CH), ln_b.reshape(1, CONV_CH), wout]
    scratch = [pltpu.VMEM((rows, POOL_CH), F32),
               pltpu.VMEM((rows, POOL_CH), F32),
               pltpu.VMEM((rows, POOL_CH - POOL_GC), F32),
               pltpu.VMEM((rows, POOL_CH - 2 * POOL_GC), F32),
               pltpu.VMEM((8, rows, CONV_CH), F32),
               pltpu.VMEM((CONV_CH // 128, 2 * rows, 128), F32),
               pltpu.VMEM((TM_MIX, CONV_CH), F32)]
    return _call(_make_pool_conv_kernel(t_len // TM_MIX, len(next_weights)), "pool_conv_mixer",
                 TM_MIX, in_specs, args, next_weights, scratch)


def _make_sgu_kernel(n_cast):
    def kern(*refs):
        ins, srcs, o_ref, dsts, (zv_ref, u_ref, v_ref, gated_ref) = _split_refs(refs, 8, n_cast)
        x_ref, g_ref, win_ref, lg_ref, lb_ref, ws_ref, bs_ref, wout_ref = ins
        n_chunks = TM_SGU // CHUNK
        n_cols = SGU_CH // MXU_TILE
        _cast_step(pl.program_id(0), srcs, dsts)

        x = x_ref[...]
        xn = _rms(x, g_ref[...]).astype(BF16)
        for c in range(n_cols):
            cols = slice(c * MXU_TILE, (c + 1) * MXU_TILE)
            w_cols = slice(SGU_CH + c * MXU_TILE, SGU_CH + (c + 1) * MXU_TILE)
            zv_ref[:, cols] = _gelu(jnp.dot(xn, win_ref[:, w_cols], preferred_element_type=F32))
        heads_per_tile = MXU_TILE // SGU_HC
        for c in range(n_cols):
            cols = slice(c * MXU_TILE, (c + 1) * MXU_TILE)
            uc = _gelu(jnp.dot(xn, win_ref[:, cols], preferred_element_type=F32))
            for k in range(heads_per_tile):
                u_ref[c * heads_per_tile + k] = uc[:, k * SGU_HC:(k + 1) * SGU_HC]
        v = _layer_norm(zv_ref[...], lg_ref[...], lb_ref[...]).astype(BF16)
        for hd in range(SGU_HEADS):
            v_ref[hd] = v[:, hd * SGU_HC:(hd + 1) * SGU_HC]

        row = lax.broadcasted_iota(jnp.int32, (CHUNK, CHUNK), 0)
        col = lax.broadcasted_iota(jnp.int32, (CHUNK, CHUNK), 1)
        mask = (col <= row).astype(F32)
        for hd in range(SGU_HEADS):
            lanes = slice(hd * SGU_HC, (hd + 1) * SGU_HC)
            w = (ws_ref[hd] * mask).astype(BF16)
            rhs = jnp.concatenate(
                [v_ref[hd, n * CHUNK:(n + 1) * CHUNK, :] for n in range(n_chunks)], axis=-1)
            r = jnp.dot(w, rhs, preferred_element_type=F32)
            for n in range(n_chunks):
                rows_n = slice(n * CHUNK, (n + 1) * CHUNK)
                vo = r[:, n * SGU_HC:(n + 1) * SGU_HC] + bs_ref[:, lanes]
                gated_ref[rows_n, lanes] = (u_ref[hd, rows_n, :] * vo).astype(BF16)
        o_ref[...] = x + jnp.dot(gated_ref[...], wout_ref[...], preferred_element_type=F32)

    return kern


def _sgu_mixer(x2, gain, win, wout, ln_g, ln_b, w_s, b_s, next_weights):
    bias = jnp.repeat(b_s.T, SGU_HC, axis=1)
    in_specs = [_tile_spec(TM_SGU), _const_spec((1, D_MODEL)), _const_spec(win.shape),
                _const_spec((1, SGU_CH)), _const_spec((1, SGU_CH)),
                _const_spec((SGU_HEADS, CHUNK, CHUNK)), _const_spec((CHUNK, SGU_CH)),
                _const_spec(wout.shape)]
    args = [x2, gain.reshape(1, D_MODEL), win, ln_g.reshape(1, SGU_CH), ln_b.reshape(1, SGU_CH),
            w_s, bias, wout]
    scratch = [pltpu.VMEM((TM_SGU, SGU_CH), F32),
               pltpu.VMEM((SGU_HEADS, TM_SGU, SGU_HC), F32),
               pltpu.VMEM((SGU_HEADS, TM_SGU, SGU_HC), BF16),
               pltpu.VMEM((TM_SGU, SGU_CH), BF16)]
    return _call(_make_sgu_kernel(len(next_weights)), "sgu_mixer", TM_SGU, in_specs, args,
                 next_weights, scratch)


def kernel(x, ffn1_norm, ffn1_w_in, ffn1_w_out, mix_norm, ffn2_norm, ffn2_w_in, ffn2_w_out,
           ab_w_in, pool_w, pool_b, pool_scale, conv_w, conv_b, conv_ln_g, conv_ln_b, ab_w_out,
           sgu_w_in, sgu_ln_g, sgu_ln_b, sgu_w, sgu_b, sgu_w_out, final_norm):
    bsz, t_len, d = x.shape
    depth = ffn1_norm.shape[0]
    assert d == D_MODEL and t_len % max(TM_MIX, TM_SGU, TM_FFN) == 0
    assert (bsz * t_len) // max(TM_MIX, TM_SGU, TM_FFN) >= N_CAST_CHUNKS
    x2 = x.reshape(bsz * t_len, d)
    w = [_cast_bf16(ffn1_w_in, 0), _cast_bf16(ffn1_w_out, 0)]
    for i in range(depth):
        j = i // 2
        pool_layer = i % 2 == 0
        mixer_weights = [(ab_w_in, j), (ab_w_out, j)] if pool_layer else [(sgu_w_in, j), (sgu_w_out, j)]
        x2, w = _ffn(x2, ffn1_norm[i], w[0], w[1], mixer_weights + [(ffn2_w_in, i), (ffn2_w_out, i)])
        w_mix, w_ffn2 = w[:2], w[2:]
        if pool_layer:
            x2, _ = _pool_conv_mixer(x2, t_len, mix_norm[i], w_mix[0], w_mix[1], pool_w[j], pool_b[j],
                                     pool_scale[j], conv_w[j], conv_b[j], conv_ln_g[j], conv_ln_b[j], [])
        else:
            x2, _ = _sgu_mixer(x2, mix_norm[i], w_mix[0], w_mix[1], sgu_ln_g[j], sgu_ln_b[j], sgu_w[j],
                               sgu_b[j], [])
        last = i == depth - 1
        next_ffn1 = [] if last else [(ffn1_w_in, i + 1), (ffn1_w_out, i + 1)]
        x2, w = _ffn(x2, ffn2_norm[i], w_ffn2[0], w_ffn2[1], next_ffn1, final_norm if last else None)
    return x2.reshape(bsz, t_len, d)
```

```python
import jax
import jax.numpy as jnp
from jax import lax
from jax.experimental import pallas as pl
from jax.experimental.pallas import tpu as pltpu

D_MODEL = 1024
D_FF = 2816
POOL_CH = 512
POOL_WINDOWS = (2, 4, 8, 16)
POOL_GC = 128
CONV_CH = 512
CONV_WIDTH = 31
AB_IN = POOL_CH + 2 * CONV_CH
SGU_CH = 1024
SGU_HEADS = 8
SGU_HC = 128
CHUNK = 128
EPS = 1e-6

MXU_TILE = 256
FF_CHUNK = MXU_TILE
N_FF_CHUNKS = D_FF // FF_CHUNK

TM_FFN = 1024
TM_MIX = 1024
TM_SGU = 1024

HALO = 32
CONV_RB = 64

N_CAST_CHUNKS = 16
N_CAST_CHUNKS_ALONE = 4
SUBLANES = 8
BF16_SUBLANES = 16

VMEM_LIMIT_BYTES = 56 * 1024 * 1024

F32 = jnp.float32
BF16 = jnp.bfloat16


def _rms(x, g):
    return x * lax.rsqrt(jnp.mean(x * x, axis=-1, keepdims=True) + EPS) * g


def _norm_parts(x, g):
    rinv = lax.rsqrt(jnp.mean(x * x, axis=-1, keepdims=True) + EPS)
    return (x * g).astype(BF16), rinv


def _layer_norm(x, g, b):
    mu = jnp.mean(x, axis=-1, keepdims=True)
    xc = x - mu
    var = jnp.mean(xc * xc, axis=-1, keepdims=True)
    return xc * lax.rsqrt(var + EPS) * g + b


def _sublane_groups(a):
    rows, lanes = a.shape
    return a.reshape(rows // SUBLANES, SUBLANES, lanes)


def _rows_later(a, d):
    a3 = _sublane_groups(a)
    rolled = pltpu.roll(a3, SUBLANES - d, axis=1)
    sub = lax.broadcasted_iota(jnp.int32, (1, SUBLANES, 1), 1)
    out = jnp.where(sub < SUBLANES - d, rolled[:-1], rolled[1:])
    return out.reshape(a.shape[0] - SUBLANES, a.shape[1])


def _rows_earlier(a, d):
    a3 = _sublane_groups(a)
    rolled = pltpu.roll(a3, d, axis=1)
    sub = lax.broadcasted_iota(jnp.int32, (1, SUBLANES, 1), 1)
    out = jnp.where(sub >= d, rolled[1:], rolled[:-1])
    return out.reshape(a.shape[0] - SUBLANES, a.shape[1])


def _put_rows(buf, slab, start, value):
    buf[slab, pl.ds(2 * start, value.shape[0], stride=2), :] = value


def _get_rows(buf, slab, start, n):
    return buf[slab, pl.ds(2 * start, n, stride=2), :]


def _gelu(x):
    return 0.5 * x * (1.0 + lax.erf(x * (2.0 ** -0.5)))


def _const_spec(shape):
    nd = len(shape)
    return pl.BlockSpec(shape, lambda *_: (0,) * nd, pipeline_mode=pl.Buffered(1))


def _tile_spec(tm):
    return pl.BlockSpec((tm, D_MODEL), lambda i: (i, 0))


def _cast_plan(w_stack, layer, n_chunks=N_CAST_CHUNKS):
    _, r, c = w_stack.shape
    assert r % (n_chunks * BF16_SUBLANES) == 0
    rows = r // n_chunks
    chunk = lambda i: jnp.minimum(i, n_chunks - 1)
    src = pl.BlockSpec((None, rows, c), lambda i: (layer, chunk(i), 0))
    dst = pl.BlockSpec((rows, c), lambda i: (chunk(i), 0))
    return src, dst, jax.ShapeDtypeStruct((r, c), BF16)


def _cast_step(step, srcs, dsts):
    @pl.when(step < N_CAST_CHUNKS)
    def _():
        for s, d in zip(srcs, dsts):
            d[...] = s[...].astype(BF16)


def _cast_kernel(s_ref, d_ref):
    d_ref[...] = s_ref[...].astype(BF16)


def _cast_bf16(w_stack, layer):
    src, dst, shape = _cast_plan(w_stack, layer, N_CAST_CHUNKS_ALONE)
    return pl.pallas_call(
        _cast_kernel, grid=(N_CAST_CHUNKS_ALONE,), in_specs=[src], out_specs=dst, out_shape=shape,
        compiler_params=pltpu.CompilerParams(
            dimension_semantics=("arbitrary",), vmem_limit_bytes=VMEM_LIMIT_BYTES),
        name="cast_bf16")(w_stack)


def _split_refs(refs, n_in, n_cast):
    bounds = [0, n_in, n_in + n_cast, n_in + n_cast + 1, n_in + 2 * n_cast + 1, len(refs)]
    ins, srcs, (o_ref,), dsts, scratch = [refs[lo:hi] for lo, hi in zip(bounds[:-1], bounds[1:])]
    return ins, srcs, o_ref, dsts, scratch


def _call(kern, name, tm, in_specs, args, next_weights, scratch_shapes):
    n_tok = args[0].shape[0]
    plans = [_cast_plan(w, layer) for w, layer in next_weights]
    outs = pl.pallas_call(
        kern,
        grid=(n_tok // tm,),
        in_specs=in_specs + [p[0] for p in plans],
        out_specs=[_tile_spec(tm)] + [p[1] for p in plans],
        out_shape=[jax.ShapeDtypeStruct((n_tok, D_MODEL), F32)] + [p[2] for p in plans],
        scratch_shapes=scratch_shapes,
        compiler_params=pltpu.CompilerParams(
            dimension_semantics=("arbitrary",), vmem_limit_bytes=VMEM_LIMIT_BYTES),
        name=name,
    )(*args, *[w for w, _ in next_weights])
    return outs[0], outs[1:]


def _make_ffn_kernel(final, n_cast):
    n_in = 5 if final else 4

    def kern(*refs):
        ins, srcs, o_ref, dsts, (a_ref,) = _split_refs(refs, n_in, n_cast)
        x_ref, g_ref, win_ref, wout_ref = ins[:4]
        _cast_step(pl.program_id(0), srcs, dsts)
        x = x_ref[...]
        xg, rinv = _norm_parts(x, g_ref[...])
        for c in range(N_FF_CHUNKS):
            cols = slice(c * FF_CHUNK, (c + 1) * FF_CHUNK)
            up_cols = slice(D_FF + c * FF_CHUNK, D_FF + (c + 1) * FF_CHUNK)
            gate = jnp.dot(xg, win_ref[:, cols], preferred_element_type=F32) * rinv
            up = jnp.dot(xg, win_ref[:, up_cols], preferred_element_type=F32) * rinv
            a_ref[:, cols] = (jax.nn.silu(gate) * up).astype(BF16)
        y = jnp.dot(a_ref[...], wout_ref[...], preferred_element_type=F32)
        out = x + 0.5 * y
        if final:
            out = _rms(out, ins[4][...])
        o_ref[...] = out

    return kern


def _ffn(x2, gain, win, wout, next_weights, final_gain=None):
    in_specs = [_tile_spec(TM_FFN), _const_spec((1, D_MODEL)), _const_spec(win.shape),
                _const_spec(wout.shape)]
    args = [x2, gain.reshape(1, D_MODEL), win, wout]
    final = final_gain is not None
    if final:
        in_specs.append(_const_spec((1, D_MODEL)))
        args.append(final_gain.reshape(1, D_MODEL))
    return _call(_make_ffn_kernel(final, len(next_weights)), "ffn_final" if final else "ffn",
                 TM_FFN, in_specs, args, next_weights, [pltpu.VMEM((TM_FFN, D_FF), BF16)])


def _make_pool_conv_kernel(tiles_per_seq, n_cast):
    def kern(*refs):
        ins, srcs, o_ref, dsts, scratch = _split_refs(refs, 11, n_cast)
        (x_ref, g_ref, win_ref, pw_ref, pb_ref, ps_ref, cw_ref, cb_ref,
         lg_ref, lb_ref, wout_ref) = ins
        ubuf, s2buf, s4buf, s8buf, g2, ybuf = scratch
        step = pl.program_id(0)
        t = step % tiles_per_seq
        rows = HALO + TM_MIX
        _cast_step(step, srcs, dsts)

        @pl.when(t == 0)
        def _():
            ubuf[0:HALO, :] = jnp.zeros((HALO, POOL_CH), F32)
            for lb in range(CONV_CH // 128):
                _put_rows(g2, lb, 0, jnp.zeros((HALO, 128), F32))

        x = x_ref[...]
        xn = _rms(x, g_ref[...]).astype(BF16)
        h = jnp.dot(xn, win_ref[...], preferred_element_type=F32)
        u = h[:, :POOL_CH]
        ubuf[HALO:, :] = u
        glu = h[:, POOL_CH:POOL_CH + CONV_CH] * jax.nn.sigmoid(h[:, POOL_CH + CONV_CH:])
        for lb in range(CONV_CH // 128):
            _put_rows(g2, lb, HALO, glu[:, lb * 128:(lb + 1) * 128])

        g1 = POOL_GC
        s2buf[8:, :] = ubuf[8:, :] + _rows_earlier(ubuf[...], 1)
        s4buf[16:, :] = s2buf[16:, g1:] + _rows_earlier(s2buf[8:, g1:], 2)
        s8buf[24:, :] = s4buf[24:, g1:] + _rows_earlier(s4buf[16:, g1:], 4)
        s16 = s8buf[HALO:, g1:] + s8buf[pl.ds(HALO - 8, TM_MIX), g1:]
        sums = [s2buf[HALO:, :g1], s4buf[HALO:, :g1], s8buf[HALO:, :g1], s16]
        pos = t * TM_MIX + lax.broadcasted_iota(jnp.int32, (TM_MIX, 1), 0)
        pooled = []
        for gi, win in enumerate(POOL_WINDOWS):
            count = jnp.minimum(pos + 1, win).astype(F32)
            pooled.append(sums[gi] / count - u[:, gi * POOL_GC:(gi + 1) * POOL_GC])
        ya = []
        for p in range(2):
            pp = jnp.concatenate(pooled[2 * p:2 * p + 2], axis=-1).astype(BF16)
            ya.append(jnp.dot(pp, pw_ref[p], preferred_element_type=F32))
        ya = (jnp.concatenate(ya, axis=-1) + pb_ref[...]) * ps_ref[...]

        taps_by_shift = [[] for _ in range(8)]
        for k in range(CONV_WIDTH):
            q, r = divmod(HALO - (CONV_WIDTH - 1) + k, 8)
            taps_by_shift[r].append((k, q))

        def conv_rows(rb, carry):
            base = pl.multiple_of(rb * CONV_RB, CONV_RB)
            for lb in range(CONV_CH // 128):
                lanes = slice(lb * 128, (lb + 1) * 128)
                acc = None
                for r, taps in enumerate(taps_by_shift):
                    span = CONV_RB + 8 * max(q for _, q in taps)
                    slab = _get_rows(g2, lb, base + r, span)
                    for k, q in taps:
                        term = slab[8 * q:8 * q + CONV_RB] * cw_ref[k:k + 1, lanes]
                        acc = term if acc is None else acc + term
                ybuf[pl.ds(base, CONV_RB), lanes] = acc
            return carry

        lax.fori_loop(0, TM_MIX // CONV_RB, conv_rows, 0)
        y = ybuf[...] + cb_ref[...]
        yb = jax.nn.silu(_layer_norm(y, lg_ref[...], lb_ref[...]))

        cat = jnp.concatenate([ya.astype(BF16), yb.astype(BF16)], axis=-1)
        o_ref[...] = x + jnp.dot(cat, wout_ref[...], preferred_element_type=F32)

        ubuf[0:HALO, :] = ubuf[TM_MIX:, :]
        for lb in range(CONV_CH // 128):
            _put_rows(g2, lb, 0, _get_rows(g2, lb, TM_MIX, HALO))

    return kern


def _pool_conv_mixer(x2, t_len, gain, win, wout, pool_w, pool_b, pool_scale, conv_w, conv_b,
                     ln_g, ln_b, next_weights):
    rows = HALO + TM_MIX
    pw = pool_w.astype(BF16)
    z = jnp.zeros((POOL_GC, POOL_GC), BF16)
    pw_bd = jnp.stack([
        jnp.block([[pw[0], z], [z, pw[1]]]),
        jnp.block([[pw[2], z], [z, pw[3]]]),
    ])
    row = lambda n: _const_spec((1, n))
    in_specs = [_tile_spec(TM_MIX), row(D_MODEL), _const_spec(win.shape), _const_spec(pw_bd.shape),
                row(POOL_CH), row(POOL_CH), _const_spec((CONV_WIDTH, CONV_CH)), row(CONV_CH),
                row(CONV_CH), row(CONV_CH), _const_spec(wout.shape)]
    args = [x2, gain.reshape(1, D_MODEL), win, pw_bd, pool_b.reshape(1, POOL_CH),
            pool_scale.reshape(1, POOL_CH), conv_w, conv_b.reshape(1, CONV_CH),
            ln_g.reshape(1, CONV_CH), ln_b.reshape(1, CONV_CH), wout]
    scratch = [pltpu.VMEM((rows, POOL_CH), F32),
               pltpu.VMEM((rows, POOL_CH), F32),
               pltpu.VMEM((rows, POOL_CH - POOL_GC), F32),
               pltpu.VMEM((rows, POOL_CH - 2 * POOL_GC), F32),
               pltpu.VMEM((CONV_CH // 128, 2 * rows, 128), F32),
               pltpu.VMEM((TM_MIX, CONV_CH), F32)]
    return _call(_make_pool_conv_kernel(t_len // TM_MIX, len(next_weights)), "pool_conv_mixer",
                 TM_MIX, in_specs, args, next_weights, scratch)


def _make_sgu_kernel(n_cast):
    def kern(*refs):
        ins, srcs, o_ref, dsts, (zv_ref, u_ref, v_ref, gated_ref) = _split_refs(refs, 8, n_cast)
        x_ref, g_ref, win_ref, lg_ref, lb_ref, ws_ref, bs_ref, wout_ref = ins
        n_chunks = TM_SGU // CHUNK
        n_cols = SGU_CH // MXU_TILE
        _cast_step(pl.program_id(0), srcs, dsts)

        x = x_ref[...]
        xn = _rms(x, g_ref[...]).astype(BF16)
        for c in range(n_cols):
            cols = slice(c * MXU_TILE, (c + 1) * MXU_TILE)
            w_cols = slice(SGU_CH + c * MXU_TILE, SGU_CH + (c + 1) * MXU_TILE)
            zv_ref[:, cols] = _gelu(jnp.dot(xn, win_ref[:, w_cols], preferred_element_type=F32))
        heads_per_tile = MXU_TILE // SGU_HC
        for c in range(n_cols):
            cols = slice(c * MXU_TILE, (c + 1) * MXU_TILE)
            uc = _gelu(jnp.dot(xn, win_ref[:, cols], preferred_element_type=F32))
            for k in range(heads_per_tile):
                u_ref[c * heads_per_tile + k] = uc[:, k * SGU_HC:(k + 1) * SGU_HC]
        v = _layer_norm(zv_ref[...], lg_ref[...], lb_ref[...]).astype(BF16)
        for hd in range(SGU_HEADS):
            v_ref[hd] = v[:, hd * SGU_HC:(hd + 1) * SGU_HC]

        row = lax.broadcasted_iota(jnp.int32, (CHUNK, CHUNK), 0)
        col = lax.broadcasted_iota(jnp.int32, (CHUNK, CHUNK), 1)
        mask = (col <= row).astype(F32)
        for hd in range(SGU_HEADS):
            lanes = slice(hd * SGU_HC, (hd + 1) * SGU_HC)
            w = (ws_ref[hd] * mask).astype(BF16)
            rhs = jnp.concatenate(
                [v_ref[hd, n * CHUNK:(n + 1) * CHUNK, :] for n in range(n_chunks)], axis=-1)
            r = jnp.dot(w, rhs, preferred_element_type=F32)
            for n in range(n_chunks):
                rows_n = slice(n * CHUNK, (n + 1) * CHUNK)
                vo = r[:, n * SGU_HC:(n + 1) * SGU_HC] + bs_ref[:, lanes]
                gated_ref[rows_n, lanes] = (u_ref[hd, rows_n, :] * vo).astype(BF16)
        o_ref[...] = x + jnp.dot(gated_ref[...], wout_ref[...], preferred_element_type=F32)

    return kern


def _sgu_mixer(x2, gain, win, wout, ln_g, ln_b, w_s, b_s, next_weights):
    bias = jnp.repeat(b_s.T, SGU_HC, axis=1)
    in_specs = [_tile_spec(TM_SGU), _const_spec((1, D_MODEL)), _const_spec(win.shape),
                _const_spec((1, SGU_CH)), _const_spec((1, SGU_CH)),
                _const_spec((SGU_HEADS, CHUNK, CHUNK)), _const_spec((CHUNK, SGU_CH)),
                _const_spec(wout.shape)]
    args = [x2, gain.reshape(1, D_MODEL), win, ln_g.reshape(1, SGU_CH), ln_b.reshape(1, SGU_CH),
            w_s, bias, wout]
    scratch = [pltpu.VMEM((TM_SGU, SGU_CH), F32),
               pltpu.VMEM((SGU_HEADS, TM_SGU, SGU_HC), F32),
               pltpu.VMEM((SGU_HEADS, TM_SGU, SGU_HC), BF16),
               pltpu.VMEM((TM_SGU, SGU_CH), BF16)]
    return _call(_make_sgu_kernel(len(next_weights)), "sgu_mixer", TM_SGU, in_specs, args,
                 next_weights, scratch)


def kernel(x, ffn1_norm, ffn1_w_in, ffn1_w_out, mix_norm, ffn2_norm, ffn2_w_in, ffn2_w_out,
           ab_w_in, pool_w, pool_b, pool_scale, conv_w, conv_b, conv_ln_g, conv_ln_b, ab_w_out,
           sgu_w_in, sgu_ln_g, sgu_ln_b, sgu_w, sgu_b, sgu_w_out, final_norm):
    bsz, t_len, d = x.shape
    depth = ffn1_norm.shape[0]
    assert d == D_MODEL and t_len % max(TM_MIX, TM_SGU, TM_FFN) == 0
    assert (bsz * t_len) // max(TM_MIX, TM_SGU, TM_FFN) >= N_CAST_CHUNKS
    x2 = x.reshape(bsz * t_len, d)
    w = [_cast_bf16(ffn1_w_in, 0), _cast_bf16(ffn1_w_out, 0)]
    for i in range(depth):
        j = i // 2
        pool_layer = i % 2 == 0
        mixer_weights = [(ab_w_in, j), (ab_w_out, j)] if pool_layer else [(sgu_w_in, j), (sgu_w_out, j)]
        x2, w = _ffn(x2, ffn1_norm[i], w[0], w[1], mixer_weights + [(ffn2_w_in, i), (ffn2_w_out, i)])
        w_mix, w_ffn2 = w[:2], w[2:]
        if pool_layer:
            x2, _ = _pool_conv_mixer(x2, t_len, mix_norm[i], w_mix[0], w_mix[1], pool_w[j], pool_b[j],
                                     pool_scale[j], conv_w[j], conv_b[j], conv_ln_g[j], conv_ln_b[j], [])
        else:
            x2, _ = _sgu_mixer(x2, mix_norm[i], w_mix[0], w_mix[1], sgu_ln_g[j], sgu_ln_b[j], sgu_w[j],
                               sgu_b[j], [])
        last = i == depth - 1
        next_ffn1 = [] if last else [(ffn1_w_in, i + 1), (ffn1_w_out, i + 1)]
        x2, w = _ffn(x2, ffn2_norm[i], w_ffn2[0], w_ffn2[1], next_ffn1, final_norm if last else None)
    return x2.reshape(bsz, t_len, d)
```

```python
import jax
import jax.numpy as jnp
from jax import lax
from jax.experimental import pallas as pl
from jax.experimental.pallas import tpu as pltpu

D_MODEL = 1024
D_FF = 2816
POOL_CH = 512
POOL_WINDOWS = (2, 4, 8, 16)
POOL_GC = 128
CONV_CH = 512
CONV_WIDTH = 31
AB_IN = POOL_CH + 2 * CONV_CH
SGU_CH = 1024
SGU_HEADS = 8
SGU_HC = 128
CHUNK = 128
EPS = 1e-6

MXU_TILE = 256
FF_CHUNK = MXU_TILE
N_FF_CHUNKS = D_FF // FF_CHUNK

TM_FFN = 1024
TM_MIX = 1024
TM_SGU = 1024

HALO = 32
CONV_RB = 64

N_CAST_CHUNKS = 16
N_CAST_CHUNKS_ALONE = 4
SUBLANES = 8
BF16_SUBLANES = 16

VMEM_LIMIT_BYTES = 56 * 1024 * 1024

F32 = jnp.float32
BF16 = jnp.bfloat16


def _rms(x, g):
    return x * lax.rsqrt(jnp.mean(x * x, axis=-1, keepdims=True) + EPS) * g


def _norm_parts(x, g):
    rinv = lax.rsqrt(jnp.mean(x * x, axis=-1, keepdims=True) + EPS)
    return (x * g).astype(BF16), rinv


def _layer_norm(x, g, b):
    mu = jnp.mean(x, axis=-1, keepdims=True)
    xc = x - mu
    var = jnp.mean(xc * xc, axis=-1, keepdims=True)
    return xc * lax.rsqrt(var + EPS) * g + b


def _sublane_groups(a):
    rows, lanes = a.shape
    return a.reshape(rows // SUBLANES, SUBLANES, lanes)


def _rows_earlier(a, d):
    a3 = _sublane_groups(a)
    rolled = pltpu.roll(a3, d, axis=1)
    sub = lax.broadcasted_iota(jnp.int32, (1, SUBLANES, 1), 1)
    out = jnp.where(sub >= d, rolled[1:], rolled[:-1])
    return out.reshape(a.shape[0] - SUBLANES, a.shape[1])


def _put_rows(buf, slab, start, value):
    buf[slab, pl.ds(2 * start, value.shape[0], stride=2), :] = value


def _get_rows(buf, slab, start, n):
    return buf[slab, pl.ds(2 * start, n, stride=2), :]


def _gelu(x):
    return 0.5 * x * (1.0 + lax.erf(x * (2.0 ** -0.5)))


def _const_spec(shape):
    nd = len(shape)
    return pl.BlockSpec(shape, lambda *_: (0,) * nd, pipeline_mode=pl.Buffered(1))


def _tile_spec(tm):
    return pl.BlockSpec((tm, D_MODEL), lambda i: (i, 0))


def _cast_plan(w_stack, layer, n_chunks=N_CAST_CHUNKS):
    _, r, c = w_stack.shape
    assert r % (n_chunks * BF16_SUBLANES) == 0
    rows = r // n_chunks
    chunk = lambda i: jnp.minimum(i, n_chunks - 1)
    src = pl.BlockSpec((None, rows, c), lambda i: (layer, chunk(i), 0))
    dst = pl.BlockSpec((rows, c), lambda i: (chunk(i), 0))
    return src, dst, jax.ShapeDtypeStruct((r, c), BF16)


def _cast_step(step, srcs, dsts):
    @pl.when(step < N_CAST_CHUNKS)
    def _():
        for s, d in zip(srcs, dsts):
            d[...] = s[...].astype(BF16)


def _cast_kernel(s_ref, d_ref):
    d_ref[...] = s_ref[...].astype(BF16)


def _cast_bf16(w_stack, layer):
    src, dst, shape = _cast_plan(w_stack, layer, N_CAST_CHUNKS_ALONE)
    return pl.pallas_call(
        _cast_kernel, grid=(N_CAST_CHUNKS_ALONE,), in_specs=[src], out_specs=dst, out_shape=shape,
        compiler_params=pltpu.CompilerParams(
            dimension_semantics=("arbitrary",), vmem_limit_bytes=VMEM_LIMIT_BYTES),
        name="cast_bf16")(w_stack)


def _split_refs(refs, n_in, n_cast):
    bounds = [0, n_in, n_in + n_cast, n_in + n_cast + 1, n_in + 2 * n_cast + 1, len(refs)]
    ins, srcs, (o_ref,), dsts, scratch = [refs[lo:hi] for lo, hi in zip(bounds[:-1], bounds[1:])]
    return ins, srcs, o_ref, dsts, scratch


def _call(kern, name, tm, in_specs, args, next_weights, scratch_shapes):
    n_tok = args[0].shape[0]
    plans = [_cast_plan(w, layer) for w, layer in next_weights]
    outs = pl.pallas_call(
        kern,
        grid=(n_tok // tm,),
        in_specs=in_specs + [p[0] for p in plans],
        out_specs=[_tile_spec(tm)] + [p[1] for p in plans],
        out_shape=[jax.ShapeDtypeStruct((n_tok, D_MODEL), F32)] + [p[2] for p in plans],
        scratch_shapes=scratch_shapes,
        compiler_params=pltpu.CompilerParams(
            dimension_semantics=("arbitrary",), vmem_limit_bytes=VMEM_LIMIT_BYTES),
        name=name,
    )(*args, *[w for w, _ in next_weights])
    return outs[0], outs[1:]


def _make_ffn_kernel(final, n_cast):
    n_in = 5 if final else 4

    def kern(*refs):
        ins, srcs, o_ref, dsts, (a_ref,) = _split_refs(refs, n_in, n_cast)
        x_ref, g_ref, win_ref, wout_ref = ins[:4]
        _cast_step(pl.program_id(0), srcs, dsts)
        x = x_ref[...]
        xg, rinv = _norm_parts(x, g_ref[...])
        for c in range(N_FF_CHUNKS):
            cols = slice(c * FF_CHUNK, (c + 1) * FF_CHUNK)
            up_cols = slice(D_FF + c * FF_CHUNK, D_FF + (c + 1) * FF_CHUNK)
            gate = jnp.dot(xg, win_ref[:, cols], preferred_element_type=F32) * rinv
            up = jnp.dot(xg, win_ref[:, up_cols], preferred_element_type=F32) * rinv
            a_ref[:, cols] = (jax.nn.silu(gate) * up).astype(BF16)
        y = jnp.dot(a_ref[...], wout_ref[...], preferred_element_type=F32)
        out = x + 0.5 * y
        if final:
            out = _rms(out, ins[4][...])
        o_ref[...] = out

    return kern


def _ffn(x2, gain, win, wout, next_weights, final_gain=None):
    in_specs = [_tile_spec(TM_FFN), _const_spec((1, D_MODEL)), _const_spec(win.shape),
                _const_spec(wout.shape)]
    args = [x2, gain.reshape(1, D_MODEL), win, wout]
    final = final_gain is not None
    if final:
        in_specs.append(_const_spec((1, D_MODEL)))
        args.append(final_gain.reshape(1, D_MODEL))
    return _call(_make_ffn_kernel(final, len(next_weights)), "ffn_final" if final else "ffn",
                 TM_FFN, in_specs, args, next_weights, [pltpu.VMEM((TM_FFN, D_FF), BF16)])


def _make_pool_conv_kernel(tiles_per_seq, n_cast):
    def kern(*refs):
        ins, srcs, o_ref, dsts, scratch = _split_refs(refs, 11, n_cast)
        (x_ref, g_ref, win_ref, pw_ref, pb_ref, ps_ref, cw_ref, cb_ref,
         lg_ref, lb_ref, wout_ref) = ins
        ubuf, s2buf, s4buf, s8buf, g2, ybuf = scratch
        step = pl.program_id(0)
        t = step % tiles_per_seq
        rows = HALO + TM_MIX
        _cast_step(step, srcs, dsts)

        @pl.when(t == 0)
        def _():
            ubuf[0:HALO, :] = jnp.zeros((HALO, POOL_CH), F32)
            for lb in range(CONV_CH // 128):
                _put_rows(g2, lb, 0, jnp.zeros((HALO, 128), F32))

        x = x_ref[...]
        xn = _rms(x, g_ref[...]).astype(BF16)
        gate = jnp.dot(xn, win_ref[:, POOL_CH + CONV_CH:], preferred_element_type=F32)
        sig = jax.nn.sigmoid(gate)
        u = jnp.dot(xn, win_ref[:, :POOL_CH], preferred_element_type=F32)
        ubuf[HALO:, :] = u
        glu = jnp.dot(xn, win_ref[:, POOL_CH:POOL_CH + CONV_CH], preferred_element_type=F32) * sig
        for lb in range(CONV_CH // 128):
            _put_rows(g2, lb, HALO, glu[:, lb * 128:(lb + 1) * 128])

        g1 = POOL_GC
        s2buf[8:, :] = ubuf[8:, :] + _rows_earlier(ubuf[...], 1)
        s4buf[16:, :] = s2buf[16:, g1:] + _rows_earlier(s2buf[8:, g1:], 2)
        s8buf[24:, :] = s4buf[24:, g1:] + _rows_earlier(s4buf[16:, g1:], 4)
        s16 = s8buf[HALO:, g1:] + s8buf[pl.ds(HALO - 8, TM_MIX), g1:]
        sums = [s2buf[HALO:, :g1], s4buf[HALO:, :g1], s8buf[HALO:, :g1], s16]
        pos = t * TM_MIX + lax.broadcasted_iota(jnp.int32, (TM_MIX, 1), 0)
        pooled = []
        for gi, win in enumerate(POOL_WINDOWS):
            count = jnp.minimum(pos + 1, win).astype(F32)
            pooled.append(sums[gi] / count - u[:, gi * POOL_GC:(gi + 1) * POOL_GC])
        ya = []
        for p in range(2):
            pp = jnp.concatenate(pooled[2 * p:2 * p + 2], axis=-1).astype(BF16)
            ya.append(jnp.dot(pp, pw_ref[p], preferred_element_type=F32))
        ya = (jnp.concatenate(ya, axis=-1) + pb_ref[...]) * ps_ref[...]

        taps_by_shift = [[] for _ in range(8)]
        for k in range(CONV_WIDTH):
            q, r = divmod(HALO - (CONV_WIDTH - 1) + k, 8)
            taps_by_shift[r].append((k, q))

        def conv_rows(rb, carry):
            base = pl.multiple_of(rb * CONV_RB, CONV_RB)
            for lb in range(CONV_CH // 128):
                lanes = slice(lb * 128, (lb + 1) * 128)
                acc = None
                for r, taps in enumerate(taps_by_shift):
                    span = CONV_RB + 8 * max(q for _, q in taps)
                    slab = _get_rows(g2, lb, base + r, span)
                    for k, q in taps:
                        term = slab[8 * q:8 * q + CONV_RB] * cw_ref[k:k + 1, lanes]
                        acc = term if acc is None else acc + term
                ybuf[pl.ds(base, CONV_RB), lanes] = acc
            return carry

        lax.fori_loop(0, TM_MIX // CONV_RB, conv_rows, 0)
        y = ybuf[...] + cb_ref[...]
        yb = jax.nn.silu(_layer_norm(y, lg_ref[...], lb_ref[...]))

        cat = jnp.concatenate([ya.astype(BF16), yb.astype(BF16)], axis=-1)
        o_ref[...] = x + jnp.dot(cat, wout_ref[...], preferred_element_type=F32)

        ubuf[0:HALO, :] = ubuf[TM_MIX:, :]
        for lb in range(CONV_CH // 128):
            _put_rows(g2, lb, 0, _get_rows(g2, lb, TM_MIX, HALO))

    return kern


def _pool_conv_mixer(x2, t_len, gain, win, wout, pool_w, pool_b, pool_scale, conv_w, conv_b,
                     ln_g, ln_b, next_weights):
    rows = HALO + TM_MIX
    pw = pool_w.astype(BF16)
    z = jnp.zeros((POOL_GC, POOL_GC), BF16)
    pw_bd = jnp.stack([
        jnp.block([[pw[0], z], [z, pw[1]]]),
        jnp.block([[pw[2], z], [z, pw[3]]]),
    ])
    row = lambda n: _const_spec((1, n))
    in_specs = [_tile_spec(TM_MIX), row(D_MODEL), _const_spec(win.shape), _const_spec(pw_bd.shape),
                row(POOL_CH), row(POOL_CH), _const_spec((CONV_WIDTH, CONV_CH)), row(CONV_CH),
                row(CONV_CH), row(CONV_CH), _const_spec(wout.shape)]
    args = [x2, gain.reshape(1, D_MODEL), win, pw_bd, pool_b.reshape(1, POOL_CH),
            pool_scale.reshape(1, POOL_CH), conv_w, conv_b.reshape(1, CONV_CH),
            ln_g.reshape(1, CONV_CH), ln_b.reshape(1, CONV_CH), wout]
    scratch = [pltpu.VMEM((rows, POOL_CH), F32),
               pltpu.VMEM((rows, POOL_CH), F32),
               pltpu.VMEM((rows, POOL_CH - POOL_GC), F32),
               pltpu.VMEM((rows, POOL_CH - 2 * POOL_GC), F32),
               pltpu.VMEM((CONV_CH // 128, 2 * rows, 128), F32),
               pltpu.VMEM((TM_MIX, CONV_CH), F32)]
    return _call(_make_pool_conv_kernel(t_len // TM_MIX, len(next_weights)), "pool_conv_mixer",
                 TM_MIX, in_specs, args, next_weights, scratch)


def _make_sgu_kernel(n_cast):
    def kern(*refs):
        ins, srcs, o_ref, dsts, (zv_ref, u_ref, v_ref, gated_ref) = _split_refs(refs, 8, n_cast)
        x_ref, g_ref, win_ref, lg_ref, lb_ref, ws_ref, bs_ref, wout_ref = ins
        n_chunks = TM_SGU // CHUNK
        n_cols = SGU_CH // MXU_TILE
        _cast_step(pl.program_id(0), srcs, dsts)

        x = x_ref[...]
        xn = _rms(x, g_ref[...]).astype(BF16)
        for c in range(n_cols):
            cols = slice(c * MXU_TILE, (c + 1) * MXU_TILE)
            w_cols = slice(SGU_CH + c * MXU_TILE, SGU_CH + (c + 1) * MXU_TILE)
            zv_ref[:, cols] = _gelu(jnp.dot(xn, win_ref[:, w_cols], preferred_element_type=F32))
        heads_per_tile = MXU_TILE // SGU_HC
        for c in range(n_cols):
            cols = slice(c * MXU_TILE, (c + 1) * MXU_TILE)
            uc = _gelu(jnp.dot(xn, win_ref[:, cols], preferred_element_type=F32))
            for k in range(heads_per_tile):
                u_ref[c * heads_per_tile + k] = uc[:, k * SGU_HC:(k + 1) * SGU_HC]
        v = _layer_norm(zv_ref[...], lg_ref[...], lb_ref[...]).astype(BF16)
        for hd in range(SGU_HEADS):
            v_ref[hd] = v[:, hd * SGU_HC:(hd + 1) * SGU_HC]

        row = lax.broadcasted_iota(jnp.int32, (CHUNK, CHUNK), 0)
        col = lax.broadcasted_iota(jnp.int32, (CHUNK, CHUNK), 1)
        mask = (col <= row).astype(F32)
        for hd in range(SGU_HEADS):
            lanes = slice(hd * SGU_HC, (hd + 1) * SGU_HC)
            w = (ws_ref[hd] * mask).astype(BF16)
            rhs = jnp.concatenate(
                [v_ref[hd, n * CHUNK:(n + 1) * CHUNK, :] for n in range(n_chunks)], axis=-1)
            r = jnp.dot(w, rhs, preferred_element_type=F32)
            for n in range(n_chunks):
                rows_n = slice(n * CHUNK, (n + 1) * CHUNK)
                vo = r[:, n * SGU_HC:(n + 1) * SGU_HC] + bs_ref[:, lanes]
                gated_ref[rows_n, lanes] = (u_ref[hd, rows_n, :] * vo).astype(BF16)
        o_ref[...] = x + jnp.dot(gated_ref[...], wout_ref[...], preferred_element_type=F32)

    return kern


def _sgu_mixer(x2, gain, win, wout, ln_g, ln_b, w_s, b_s, next_weights):
    bias = jnp.repeat(b_s.T, SGU_HC, axis=1)
    in_specs = [_tile_spec(TM_SGU), _const_spec((1, D_MODEL)), _const_spec(win.shape),
                _const_spec((1, SGU_CH)), _const_spec((1, SGU_CH)),
                _const_spec((SGU_HEADS, CHUNK, CHUNK)), _const_spec((CHUNK, SGU_CH)),
                _const_spec(wout.shape)]
    args = [x2, gain.reshape(1, D_MODEL), win, ln_g.reshape(1, SGU_CH), ln_b.reshape(1, SGU_CH),
            w_s, bias, wout]
    scratch = [pltpu.VMEM((TM_SGU, SGU_CH), F32),
               pltpu.VMEM((SGU_HEADS, TM_SGU, SGU_HC), F32),
               pltpu.VMEM((SGU_HEADS, TM_SGU, SGU_HC), BF16),
               pltpu.VMEM((TM_SGU, SGU_CH), BF16)]
    return _call(_make_sgu_kernel(len(next_weights)), "sgu_mixer", TM_SGU, in_specs, args,
                 next_weights, scratch)


def kernel(x, ffn1_norm, ffn1_w_in, ffn1_w_out, mix_norm, ffn2_norm, ffn2_w_in, ffn2_w_out,
           ab_w_in, pool_w, pool_b, pool_scale, conv_w, conv_b, conv_ln_g, conv_ln_b, ab_w_out,
           sgu_w_in, sgu_ln_g, sgu_ln_b, sgu_w, sgu_b, sgu_w_out, final_norm):
    bsz, t_len, d = x.shape
    depth = ffn1_norm.shape[0]
    assert d == D_MODEL and t_len % max(TM_MIX, TM_SGU, TM_FFN) == 0
    assert (bsz * t_len) // max(TM_MIX, TM_SGU, TM_FFN) >= N_CAST_CHUNKS
    x2 = x.reshape(bsz * t_len, d)
    w = [_cast_bf16(ffn1_w_in, 0), _cast_bf16(ffn1_w_out, 0)]
    for i in range(depth):
        j = i // 2
        pool_layer = i % 2 == 0
        mixer_weights = [(ab_w_in, j), (ab_w_out, j)] if pool_layer else [(sgu_w_in, j), (sgu_w_out, j)]
        x2, w = _ffn(x2, ffn1_norm[i], w[0], w[1], mixer_weights + [(ffn2_w_in, i), (ffn2_w_out, i)])
        w_mix, w_ffn2 = w[:2], w[2:]
        if pool_layer:
            x2, _ = _pool_conv_mixer(x2, t_len, mix_norm[i], w_mix[0], w_mix[1], pool_w[j], pool_b[j],
                                     pool_scale[j], conv_w[j], conv_b[j], conv_ln_g[j], conv_ln_b[j], [])
        else:
            x2, _ = _sgu_mixer(x2, mix_norm[i], w_mix[0], w_mix[1], sgu_ln_g[j], sgu_ln_b[j], sgu_w[j],
                               sgu_b[j], [])
        last = i == depth - 1
        next_ffn1 = [] if last else [(ffn1_w_in, i + 1), (ffn1_w_out, i + 1)]
        x2, w = _ffn(x2, ffn2_norm[i], w_ffn2[0], w_ffn2[1], next_ffn1, final_norm if last else None)
    return x2.reshape(bsz, t_len, d)
```

```python
import jax
import jax.numpy as jnp
from jax import lax
from jax.experimental import pallas as pl
from jax.experimental.pallas import tpu as pltpu

D_MODEL = 1024
D_FF = 2816
POOL_CH = 512
POOL_WINDOWS = (2, 4, 8, 16)
POOL_GC = 128
CONV_CH = 512
CONV_WIDTH = 31
AB_IN = POOL_CH + 2 * CONV_CH
SGU_CH = 1024
SGU_HEADS = 8
SGU_HC = 128
CHUNK = 128
EPS = 1e-6

MXU_TILE = 256
FF_CHUNK = MXU_TILE
N_FF_CHUNKS = D_FF // FF_CHUNK

TM_FFN = 1024
TM_MIX = 1024
TM_SGU = 1024

HALO = 32
CONV_RB = 64

N_CAST_CHUNKS = 16
N_CAST_CHUNKS_ALONE = 4
SUBLANES = 8
BF16_SUBLANES = 16

VMEM_LIMIT_BYTES = 56 * 1024 * 1024

F32 = jnp.float32
BF16 = jnp.bfloat16


def _rms(x, g):
    return x * lax.rsqrt(jnp.mean(x * x, axis=-1, keepdims=True) + EPS) * g


def _norm_parts(x, g):
    rinv = lax.rsqrt(jnp.mean(x * x, axis=-1, keepdims=True) + EPS)
    return (x * g).astype(BF16), rinv


def _layer_norm(x, g, b):
    mu = jnp.mean(x, axis=-1, keepdims=True)
    xc = x - mu
    var = jnp.mean(xc * xc, axis=-1, keepdims=True)
    return xc * lax.rsqrt(var + EPS) * g + b


def _sublane_groups(a):
    rows, lanes = a.shape
    return a.reshape(rows // SUBLANES, SUBLANES, lanes)


def _rows_earlier(a, d):
    a3 = _sublane_groups(a)
    rolled = pltpu.roll(a3, d, axis=1)
    sub = lax.broadcasted_iota(jnp.int32, (1, SUBLANES, 1), 1)
    out = jnp.where(sub >= d, rolled[1:], rolled[:-1])
    return out.reshape(a.shape[0] - SUBLANES, a.shape[1])


def _put_rows(buf, slab, start, value):
    buf[slab, pl.ds(2 * start, value.shape[0], stride=2), :] = value


def _get_rows(buf, slab, start, n):
    return buf[slab, pl.ds(2 * start, n, stride=2), :]


def _gelu(x):
    return 0.5 * x * (1.0 + lax.erf(x * (2.0 ** -0.5)))


def _const_spec(shape):
    nd = len(shape)
    return pl.BlockSpec(shape, lambda *_: (0,) * nd, pipeline_mode=pl.Buffered(1))


def _tile_spec(tm):
    return pl.BlockSpec((tm, D_MODEL), lambda i: (i, 0))


def _cast_plan(w_stack, layer, n_chunks=N_CAST_CHUNKS):
    _, r, c = w_stack.shape
    assert r % (n_chunks * BF16_SUBLANES) == 0
    rows = r // n_chunks
    chunk = lambda i: jnp.minimum(i, n_chunks - 1)
    src = pl.BlockSpec((None, rows, c), lambda i: (layer, chunk(i), 0))
    dst = pl.BlockSpec((rows, c), lambda i: (chunk(i), 0))
    return src, dst, jax.ShapeDtypeStruct((r, c), BF16)


def _cast_step(step, srcs, dsts):
    @pl.when(step < N_CAST_CHUNKS)
    def _():
        for s, d in zip(srcs, dsts):
            d[...] = s[...].astype(BF16)


def _cast_kernel(s_ref, d_ref):
    d_ref[...] = s_ref[...].astype(BF16)


def _cast_bf16(w_stack, layer):
    src, dst, shape = _cast_plan(w_stack, layer, N_CAST_CHUNKS_ALONE)
    return pl.pallas_call(
        _cast_kernel, grid=(N_CAST_CHUNKS_ALONE,), in_specs=[src], out_specs=dst, out_shape=shape,
        compiler_params=pltpu.CompilerParams(
            dimension_semantics=("arbitrary",), vmem_limit_bytes=VMEM_LIMIT_BYTES),
        name="cast_bf16")(w_stack)


def _split_refs(refs, n_in, n_cast):
    bounds = [0, n_in, n_in + n_cast, n_in + n_cast + 1, n_in + 2 * n_cast + 1, len(refs)]
    ins, srcs, (o_ref,), dsts, scratch = [refs[lo:hi] for lo, hi in zip(bounds[:-1], bounds[1:])]
    return ins, srcs, o_ref, dsts, scratch


def _call(kern, name, tm, in_specs, args, next_weights, scratch_shapes):
    n_tok = args[0].shape[0]
    plans = [_cast_plan(w, layer) for w, layer in next_weights]
    outs = pl.pallas_call(
        kern,
        grid=(n_tok // tm,),
        in_specs=in_specs + [p[0] for p in plans],
        out_specs=[_tile_spec(tm)] + [p[1] for p in plans],
        out_shape=[jax.ShapeDtypeStruct((n_tok, D_MODEL), F32)] + [p[2] for p in plans],
        scratch_shapes=scratch_shapes,
        compiler_params=pltpu.CompilerParams(
            dimension_semantics=("arbitrary",), vmem_limit_bytes=VMEM_LIMIT_BYTES),
        name=name,
    )(*args, *[w for w, _ in next_weights])
    return outs[0], outs[1:]


def _make_ffn_kernel(final, n_cast):
    n_in = 5 if final else 4

    def kern(*refs):
        ins, srcs, o_ref, dsts, (a_ref,) = _split_refs(refs, n_in, n_cast)
        x_ref, g_ref, win_ref, wout_ref = ins[:4]
        _cast_step(pl.program_id(0), srcs, dsts)
        x = x_ref[...]
        xg, rinv = _norm_parts(x, g_ref[...])
        for c in range(N_FF_CHUNKS):
            cols = slice(c * FF_CHUNK, (c + 1) * FF_CHUNK)
            up_cols = slice(D_FF + c * FF_CHUNK, D_FF + (c + 1) * FF_CHUNK)
            gate = jnp.dot(xg, win_ref[:, cols], preferred_element_type=F32) * rinv
            up = jnp.dot(xg, win_ref[:, up_cols], preferred_element_type=F32) * rinv
            a_ref[:, cols] = (jax.nn.silu(gate) * up).astype(BF16)
        y = jnp.dot(a_ref[...], wout_ref[...], preferred_element_type=F32)
        out = x + 0.5 * y
        if final:
            out = _rms(out, ins[4][...])
        o_ref[...] = out

    return kern


def _ffn(x2, gain, win, wout, next_weights, final_gain=None):
    in_specs = [_tile_spec(TM_FFN), _const_spec((1, D_MODEL)), _const_spec(win.shape),
                _const_spec(wout.shape)]
    args = [x2, gain.reshape(1, D_MODEL), win, wout]
    final = final_gain is not None
    if final:
        in_specs.append(_const_spec((1, D_MODEL)))
        args.append(final_gain.reshape(1, D_MODEL))
    return _call(_make_ffn_kernel(final, len(next_weights)), "ffn_final" if final else "ffn",
                 TM_FFN, in_specs, args, next_weights, [pltpu.VMEM((TM_FFN, D_FF), BF16)])


def _make_pool_conv_kernel(tiles_per_seq, n_cast):
    def kern(*refs):
        ins, srcs, o_ref, dsts, scratch = _split_refs(refs, 11, n_cast)
        (x_ref, g_ref, win_ref, pw_ref, pb_ref, ps_ref, cw_ref, cb_ref,
         lg_ref, lb_ref, wout_ref) = ins
        ubuf, s2buf, s4buf, s8buf, g2, ybuf = scratch
        step = pl.program_id(0)
        t = step % tiles_per_seq
        rows = HALO + TM_MIX
        _cast_step(step, srcs, dsts)

        @pl.when(t == 0)
        def _():
            ubuf[0:HALO, :] = jnp.zeros((HALO, POOL_CH), F32)
            for lb in range(CONV_CH // 128):
                _put_rows(g2, lb, 0, jnp.zeros((HALO, 128), F32))

        x = x_ref[...]
        xn = _rms(x, g_ref[...]).astype(BF16)
        gate = jnp.dot(xn, win_ref[:, POOL_CH + CONV_CH:], preferred_element_type=F32)
        sig = jax.nn.sigmoid(gate)
        u = jnp.dot(xn, win_ref[:, :POOL_CH], preferred_element_type=F32)
        ubuf[HALO:, :] = u
        glu = jnp.dot(xn, win_ref[:, POOL_CH:POOL_CH + CONV_CH], preferred_element_type=F32) * sig
        for lb in range(CONV_CH // 128):
            _put_rows(g2, lb, HALO, glu[:, lb * 128:(lb + 1) * 128])

        g1 = POOL_GC
        s2buf[8:, :] = ubuf[8:, :] + _rows_earlier(ubuf[...], 1)
        s4buf[16:, :] = s2buf[16:, g1:] + _rows_earlier(s2buf[8:, g1:], 2)
        s8buf[24:, :] = s4buf[24:, g1:] + _rows_earlier(s4buf[16:, g1:], 4)
        s16 = s8buf[HALO:, g1:] + s8buf[pl.ds(HALO - 8, TM_MIX), g1:]
        sums = [s2buf[HALO:, :g1], s4buf[HALO:, :g1], s8buf[HALO:, :g1], s16]
        pos = t * TM_MIX + lax.broadcasted_iota(jnp.int32, (TM_MIX, 1), 0)
        pooled = []
        for gi, win in enumerate(POOL_WINDOWS):
            count = jnp.minimum(pos + 1, win).astype(F32)
            pooled.append(sums[gi] / count - u[:, gi * POOL_GC:(gi + 1) * POOL_GC])
        ya = []
        for p in range(2):
            pp = jnp.concatenate(pooled[2 * p:2 * p + 2], axis=-1).astype(BF16)
            ya.append(jnp.dot(pp, pw_ref[p], preferred_element_type=F32))
        ya = (jnp.concatenate(ya, axis=-1) + pb_ref[...]) * ps_ref[...]

        taps_by_shift = [[] for _ in range(8)]
        for k in range(CONV_WIDTH):
            q, r = divmod(HALO - (CONV_WIDTH - 1) + k, 8)
            taps_by_shift[r].append((k, q))

        def conv_rows(rb, carry):
            base = pl.multiple_of(rb * CONV_RB, CONV_RB)
            for lb in range(CONV_CH // 128):
                lanes = slice(lb * 128, (lb + 1) * 128)
                acc = None
                for r, taps in enumerate(taps_by_shift):
                    span = CONV_RB + 8 * max(q for _, q in taps)
                    slab = _get_rows(g2, lb, base + r, span)
                    for k, q in taps:
                        term = slab[8 * q:8 * q + CONV_RB] * cw_ref[k:k + 1, lanes]
                        acc = term if acc is None else acc + term
                ybuf[pl.ds(base, CONV_RB), lanes] = acc
            return carry

        lax.fori_loop(0, TM_MIX // CONV_RB, conv_rows, 0)
        y = ybuf[...] + cb_ref[...]
        yb = jax.nn.silu(_layer_norm(y, lg_ref[...], lb_ref[...]))

        cat = jnp.concatenate([ya.astype(BF16), yb.astype(BF16)], axis=-1)
        o_ref[...] = x + jnp.dot(cat, wout_ref[...], preferred_element_type=F32)

        ubuf[0:HALO, :] = ubuf[TM_MIX:, :]
        for lb in range(CONV_CH // 128):
            _put_rows(g2, lb, 0, _get_rows(g2, lb, TM_MIX, HALO))

    return kern


def _pool_conv_mixer(x2, t_len, gain, win, wout, pool_w, pool_b, pool_scale, conv_w, conv_b,
                     ln_g, ln_b, next_weights):
    rows = HALO + TM_MIX
    pw = pool_w.astype(BF16)
    z = jnp.zeros((POOL_GC, POOL_GC), BF16)
    pw_bd = jnp.stack([
        jnp.block([[pw[0], z], [z, pw[1]]]),
        jnp.block([[pw[2], z], [z, pw[3]]]),
    ])
    row = lambda n: _const_spec((1, n))
    in_specs = [_tile_spec(TM_MIX), row(D_MODEL), _const_spec(win.shape), _const_spec(pw_bd.shape),
                row(POOL_CH), row(POOL_CH), _const_spec((CONV_WIDTH, CONV_CH)), row(CONV_CH),
                row(CONV_CH), row(CONV_CH), _const_spec(wout.shape)]
    args = [x2, gain.reshape(1, D_MODEL), win, pw_bd, pool_b.reshape(1, POOL_CH),
            pool_scale.reshape(1, POOL_CH), conv_w, conv_b.reshape(1, CONV_CH),
            ln_g.reshape(1, CONV_CH), ln_b.reshape(1, CONV_CH), wout]
    scratch = [pltpu.VMEM((rows, POOL_CH), F32),
               pltpu.VMEM((rows, POOL_CH), F32),
               pltpu.VMEM((rows, POOL_CH - POOL_GC), F32),
               pltpu.VMEM((rows, POOL_CH - 2 * POOL_GC), F32),
               pltpu.VMEM((CONV_CH // 128, 2 * rows, 128), F32),
               pltpu.VMEM((TM_MIX, CONV_CH), F32)]
    return _call(_make_pool_conv_kernel(t_len // TM_MIX, len(next_weights)), "pool_conv_mixer",
                 TM_MIX, in_specs, args, next_weights, scratch)


def _make_sgu_kernel(n_cast):
    def kern(*refs):
        ins, srcs, o_ref, dsts, (u_ref, v_ref, gated_ref) = _split_refs(refs, 8, n_cast)
        x_ref, g_ref, win_ref, lg_ref, lb_ref, ws_ref, bs_ref, wout_ref = ins
        n_chunks = TM_SGU // CHUNK
        n_cols = SGU_CH // MXU_TILE
        _cast_step(pl.program_id(0), srcs, dsts)

        x = x_ref[...]
        xn = _rms(x, g_ref[...]).astype(BF16)
        zv = []
        for c in range(n_cols):
            w_cols = slice(SGU_CH + c * MXU_TILE, SGU_CH + (c + 1) * MXU_TILE)
            zv.append(_gelu(jnp.dot(xn, win_ref[:, w_cols], preferred_element_type=F32)))
        heads_per_tile = MXU_TILE // SGU_HC
        for c in range(n_cols):
            cols = slice(c * MXU_TILE, (c + 1) * MXU_TILE)
            uc = _gelu(jnp.dot(xn, win_ref[:, cols], preferred_element_type=F32))
            for k in range(heads_per_tile):
                u_ref[c * heads_per_tile + k] = uc[:, k * SGU_HC:(k + 1) * SGU_HC]
        v = _layer_norm(jnp.concatenate(zv, axis=-1), lg_ref[...], lb_ref[...]).astype(BF16)
        for hd in range(SGU_HEADS):
            v_ref[hd] = v[:, hd * SGU_HC:(hd + 1) * SGU_HC]

        row = lax.broadcasted_iota(jnp.int32, (CHUNK, CHUNK), 0)
        col = lax.broadcasted_iota(jnp.int32, (CHUNK, CHUNK), 1)
        mask = (col <= row).astype(F32)
        for hd in range(SGU_HEADS):
            lanes = slice(hd * SGU_HC, (hd + 1) * SGU_HC)
            w = (ws_ref[hd] * mask).astype(BF16)
            rhs = jnp.concatenate(
                [v_ref[hd, n * CHUNK:(n + 1) * CHUNK, :] for n in range(n_chunks)], axis=-1)
            r = jnp.dot(w, rhs, preferred_element_type=F32)
            for n in range(n_chunks):
                rows_n = slice(n * CHUNK, (n + 1) * CHUNK)
                vo = r[:, n * SGU_HC:(n + 1) * SGU_HC] + bs_ref[:, lanes]
                gated_ref[rows_n, lanes] = (u_ref[hd, rows_n, :] * vo).astype(BF16)
        o_ref[...] = x + jnp.dot(gated_ref[...], wout_ref[...], preferred_element_type=F32)

    return kern


def _sgu_mixer(x2, gain, win, wout, ln_g, ln_b, w_s, b_s, next_weights):
    bias = jnp.repeat(b_s.T, SGU_HC, axis=1)
    in_specs = [_tile_spec(TM_SGU), _const_spec((1, D_MODEL)), _const_spec(win.shape),
                _const_spec((1, SGU_CH)), _const_spec((1, SGU_CH)),
                _const_spec((SGU_HEADS, CHUNK, CHUNK)), _const_spec((CHUNK, SGU_CH)),
                _const_spec(wout.shape)]
    args = [x2, gain.reshape(1, D_MODEL), win, ln_g.reshape(1, SGU_CH), ln_b.reshape(1, SGU_CH),
            w_s, bias, wout]
    scratch = [pltpu.VMEM((SGU_HEADS, TM_SGU, SGU_HC), F32),
               pltpu.VMEM((SGU_HEADS, TM_SGU, SGU_HC), BF16),
               pltpu.VMEM((TM_SGU, SGU_CH), BF16)]
    return _call(_make_sgu_kernel(len(next_weights)), "sgu_mixer", TM_SGU, in_specs, args,
                 next_weights, scratch)


def kernel(x, ffn1_norm, ffn1_w_in, ffn1_w_out, mix_norm, ffn2_norm, ffn2_w_in, ffn2_w_out,
           ab_w_in, pool_w, pool_b, pool_scale, conv_w, conv_b, conv_ln_g, conv_ln_b, ab_w_out,
           sgu_w_in, sgu_ln_g, sgu_ln_b, sgu_w, sgu_b, sgu_w_out, final_norm):
    bsz, t_len, d = x.shape
    depth = ffn1_norm.shape[0]
    assert d == D_MODEL and t_len % max(TM_MIX, TM_SGU, TM_FFN) == 0
    assert (bsz * t_len) // max(TM_MIX, TM_SGU, TM_FFN) >= N_CAST_CHUNKS
    x2 = x.reshape(bsz * t_len, d)
    w = [_cast_bf16(ffn1_w_in, 0), _cast_bf16(ffn1_w_out, 0)]
    for i in range(depth):
        j = i // 2
        pool_layer = i % 2 == 0
        mixer_weights = [(ab_w_in, j), (ab_w_out, j)] if pool_layer else [(sgu_w_in, j), (sgu_w_out, j)]
        x2, w = _ffn(x2, ffn1_norm[i], w[0], w[1], mixer_weights + [(ffn2_w_in, i), (ffn2_w_out, i)])
        w_mix, w_ffn2 = w[:2], w[2:]
        if pool_layer:
            x2, _ = _pool_conv_mixer(x2, t_len, mix_norm[i], w_mix[0], w_mix[1], pool_w[j], pool_b[j],
                                     pool_scale[j], conv_w[j], conv_b[j], conv_ln_g[j], conv_ln_b[j], [])
        else:
            x2, _ = _sgu_mixer(x2, mix_norm[i], w_mix[0], w_mix[1], sgu_ln_g[j], sgu_ln_b[j], sgu_w[j],
                               sgu_b[j], [])
        last = i == depth - 1
        next_ffn1 = [] if last else [(ffn1_w_in, i + 1), (ffn1_w_out, i + 1)]
        x2, w = _ffn(x2, ffn2_norm[i], w_ffn2[0], w_ffn2[1], next_ffn1, final_norm if last else None)
    return x2.reshape(bsz, t_len, d)
```

```python
import jax
import jax.numpy as jnp
from jax import lax
from jax.experimental import pallas as pl
from jax.experimental.pallas import tpu as pltpu

D_MODEL = 1024
D_FF = 2816
POOL_CH = 512
POOL_WINDOWS = (2, 4, 8, 16)
POOL_GC = 128
CONV_CH = 512
CONV_WIDTH = 31
SGU_CH = 1024
SGU_HEADS = 8
SGU_HC = 128
CHUNK = 128
EPS = 1e-6

MXU_TILE = 256
FF_CHUNK = MXU_TILE
N_FF_CHUNKS = D_FF // FF_CHUNK

TM_FFN = 1024
TM_MIX = 1024
TM_SGU = 1024

HALO = 32
CONV_RB = 64

N_CAST_CHUNKS = 16
N_CAST_CHUNKS_ALONE = 4
LANES = 128
SUBLANES = 8
BF16_SUBLANES = 16

VMEM_LIMIT_BYTES = 56 * 1024 * 1024

F32 = jnp.float32
BF16 = jnp.bfloat16


def _rms(x, g):
    return x * lax.rsqrt(jnp.mean(x * x, axis=-1, keepdims=True) + EPS) * g


def _norm_parts(x, g):
    rinv = lax.rsqrt(jnp.mean(x * x, axis=-1, keepdims=True) + EPS)
    return (x * g).astype(BF16), rinv


def _layer_norm(x, g, b):
    mu = jnp.mean(x, axis=-1, keepdims=True)
    xc = x - mu
    var = jnp.mean(xc * xc, axis=-1, keepdims=True)
    return xc * lax.rsqrt(var + EPS) * g + b


def _sublane_groups(a):
    rows, lanes = a.shape
    return a.reshape(rows // SUBLANES, SUBLANES, lanes)


def _rows_earlier(a, d):
    a3 = _sublane_groups(a)
    rolled = pltpu.roll(a3, d, axis=1)
    sub = lax.broadcasted_iota(jnp.int32, (1, SUBLANES, 1), 1)
    out = jnp.where(sub >= d, rolled[1:], rolled[:-1])
    return out.reshape(a.shape[0] - SUBLANES, a.shape[1])


def _put_rows(buf, slab, start, value):
    buf[slab, pl.ds(2 * start, value.shape[0], stride=2), :] = value


def _get_rows(buf, slab, start, n):
    return buf[slab, pl.ds(2 * start, n, stride=2), :]


def _gelu(x):
    return 0.5 * x * (1.0 + lax.erf(x * (2.0 ** -0.5)))


def _const_spec(shape):
    nd = len(shape)
    return pl.BlockSpec(shape, lambda *_: (0,) * nd, pipeline_mode=pl.Buffered(1))


def _tile_spec(tm):
    return pl.BlockSpec((tm, D_MODEL), lambda i: (i, 0))


def _cast_plan(w_stack, layer, n_chunks=N_CAST_CHUNKS):
    _, r, c = w_stack.shape
    assert r % (n_chunks * BF16_SUBLANES) == 0
    rows = r // n_chunks
    chunk = lambda i: jnp.minimum(i, n_chunks - 1)
    src = pl.BlockSpec((None, rows, c), lambda i: (layer, chunk(i), 0))
    dst = pl.BlockSpec((rows, c), lambda i: (chunk(i), 0))
    return src, dst, jax.ShapeDtypeStruct((r, c), BF16)


def _cast_step(step, srcs, dsts):
    @pl.when(step < N_CAST_CHUNKS)
    def _():
        for s, d in zip(srcs, dsts):
            d[...] = s[...].astype(BF16)


def _cast_kernel(*refs):
    srcs, dsts = refs[:len(refs) // 2], refs[len(refs) // 2:]
    for s, d in zip(srcs, dsts):
        d[...] = s[...].astype(BF16)


def _cast_bf16(weights):
    plans = [_cast_plan(w, layer, N_CAST_CHUNKS_ALONE) for w, layer in weights]
    return pl.pallas_call(
        _cast_kernel, grid=(N_CAST_CHUNKS_ALONE,), in_specs=[p[0] for p in plans],
        out_specs=[p[1] for p in plans], out_shape=[p[2] for p in plans],
        compiler_params=pltpu.CompilerParams(
            dimension_semantics=("arbitrary",), vmem_limit_bytes=VMEM_LIMIT_BYTES),
        name="cast_bf16")(*[w for w, _ in weights])


def _split_refs(refs, n_in, n_cast):
    bounds = [0, n_in, n_in + n_cast, n_in + n_cast + 1, n_in + 2 * n_cast + 1, len(refs)]
    ins, srcs, (o_ref,), dsts, scratch = [refs[lo:hi] for lo, hi in zip(bounds[:-1], bounds[1:])]
    return ins, srcs, o_ref, dsts, scratch


def _call(kern, name, tm, in_specs, args, next_weights, scratch_shapes):
    n_tok = args[0].shape[0]
    plans = [_cast_plan(w, layer) for w, layer in next_weights]
    outs = pl.pallas_call(
        kern,
        grid=(n_tok // tm,),
        in_specs=in_specs + [p[0] for p in plans],
        out_specs=[_tile_spec(tm)] + [p[1] for p in plans],
        out_shape=[jax.ShapeDtypeStruct((n_tok, D_MODEL), F32)] + [p[2] for p in plans],
        scratch_shapes=scratch_shapes,
        compiler_params=pltpu.CompilerParams(
            dimension_semantics=("arbitrary",), vmem_limit_bytes=VMEM_LIMIT_BYTES),
        name=name,
    )(*args, *[w for w, _ in next_weights])
    return outs[0], outs[1:]


def _make_ffn_kernel(final, n_cast):
    n_in = 5 if final else 4

    def kern(*refs):
        ins, srcs, o_ref, dsts, (a_ref,) = _split_refs(refs, n_in, n_cast)
        x_ref, g_ref, win_ref, wout_ref = ins[:4]
        _cast_step(pl.program_id(0), srcs, dsts)
        x = x_ref[...]
        xg, rinv = _norm_parts(x, g_ref[...])
        for c in range(N_FF_CHUNKS):
            cols = slice(c * FF_CHUNK, (c + 1) * FF_CHUNK)
            up_cols = slice(D_FF + c * FF_CHUNK, D_FF + (c + 1) * FF_CHUNK)
            gate = jnp.dot(xg, win_ref[:, cols], preferred_element_type=F32) * rinv
            up = jnp.dot(xg, win_ref[:, up_cols], preferred_element_type=F32) * rinv
            a_ref[:, cols] = (jax.nn.silu(gate) * up).astype(BF16)
        y = jnp.dot(a_ref[...], wout_ref[...], preferred_element_type=F32)
        out = x + 0.5 * y
        if final:
            out = _rms(out, ins[4][...])
        o_ref[...] = out

    return kern


def _ffn(x2, gain, win, wout, next_weights, final_gain=None):
    in_specs = [_tile_spec(TM_FFN), _const_spec((1, D_MODEL)), _const_spec(win.shape),
                _const_spec(wout.shape)]
    args = [x2, gain.reshape(1, D_MODEL), win, wout]
    final = final_gain is not None
    if final:
        in_specs.append(_const_spec((1, D_MODEL)))
        args.append(final_gain.reshape(1, D_MODEL))
    return _call(_make_ffn_kernel(final, len(next_weights)), "ffn_final" if final else "ffn",
                 TM_FFN, in_specs, args, next_weights, [pltpu.VMEM((TM_FFN, D_FF), BF16)])


def _make_pool_conv_kernel(tiles_per_seq, n_cast):
    def kern(*refs):
        ins, srcs, o_ref, dsts, scratch = _split_refs(refs, 11, n_cast)
        (x_ref, g_ref, win_ref, pw_ref, pb_ref, ps_ref, cw_ref, cb_ref,
         lg_ref, lb_ref, wout_ref) = ins
        ubuf, s2buf, s4buf, s8buf, g2, ybuf = scratch
        step = pl.program_id(0)
        t = step % tiles_per_seq
        rows = HALO + TM_MIX
        _cast_step(step, srcs, dsts)

        @pl.when(t == 0)
        def _():
            ubuf[0:HALO, :] = jnp.zeros((HALO, POOL_CH), F32)
            for lb in range(CONV_CH // LANES):
                _put_rows(g2, lb, 0, jnp.zeros((HALO, LANES), F32))

        x = x_ref[...]
        xn = _rms(x, g_ref[...]).astype(BF16)
        gate = jnp.dot(xn, win_ref[:, POOL_CH + CONV_CH:], preferred_element_type=F32)
        sig = jax.nn.sigmoid(gate)
        u = jnp.dot(xn, win_ref[:, :POOL_CH], preferred_element_type=F32)
        ubuf[HALO:, :] = u
        glu = jnp.dot(xn, win_ref[:, POOL_CH:POOL_CH + CONV_CH], preferred_element_type=F32) * sig
        for lb in range(CONV_CH // LANES):
            _put_rows(g2, lb, HALO, glu[:, lb * LANES:(lb + 1) * LANES])

        g1 = POOL_GC
        s2buf[8:, :] = ubuf[8:, :] + _rows_earlier(ubuf[...], 1)
        s4buf[16:, :] = s2buf[16:, g1:] + _rows_earlier(s2buf[8:, g1:], 2)
        s8buf[24:, :] = s4buf[24:, g1:] + _rows_earlier(s4buf[16:, g1:], 4)
        s16 = s8buf[HALO:, g1:] + s8buf[pl.ds(HALO - 8, TM_MIX), g1:]
        sums = [s2buf[HALO:, :g1], s4buf[HALO:, :g1], s8buf[HALO:, :g1], s16]
        pos = t * TM_MIX + lax.broadcasted_iota(jnp.int32, (TM_MIX, 1), 0)
        pooled = []
        for gi, win in enumerate(POOL_WINDOWS):
            count = jnp.minimum(pos + 1, win).astype(F32)
            pooled.append(sums[gi] / count - u[:, gi * POOL_GC:(gi + 1) * POOL_GC])
        ya = []
        for p in range(2):
            pp = jnp.concatenate(pooled[2 * p:2 * p + 2], axis=-1).astype(BF16)
            ya.append(jnp.dot(pp, pw_ref[p], preferred_element_type=F32))
        ya = (jnp.concatenate(ya, axis=-1) + pb_ref[...]) * ps_ref[...]

        taps_by_shift = [[] for _ in range(8)]
        for k in range(CONV_WIDTH):
            q, r = divmod(HALO - (CONV_WIDTH - 1) + k, 8)
            taps_by_shift[r].append((k, q))

        def conv_rows(rb, carry):
            base = pl.multiple_of(rb * CONV_RB, CONV_RB)
            for lb in range(CONV_CH // LANES):
                lanes = slice(lb * LANES, (lb + 1) * LANES)
                acc = None
                for r, taps in enumerate(taps_by_shift):
                    span = CONV_RB + 8 * max(q for _, q in taps)
                    slab = _get_rows(g2, lb, base + r, span)
                    for k, q in taps:
                        term = slab[8 * q:8 * q + CONV_RB] * cw_ref[k:k + 1, lanes]
                        acc = term if acc is None else acc + term
                ybuf[pl.ds(base, CONV_RB), lanes] = acc
            return carry

        lax.fori_loop(0, TM_MIX // CONV_RB, conv_rows, 0)
        y = ybuf[...] + cb_ref[...]
        yb = jax.nn.silu(_layer_norm(y, lg_ref[...], lb_ref[...]))

        cat = jnp.concatenate([ya.astype(BF16), yb.astype(BF16)], axis=-1)
        o_ref[...] = x + jnp.dot(cat, wout_ref[...], preferred_element_type=F32)

        ubuf[0:HALO, :] = ubuf[TM_MIX:, :]
        for lb in range(CONV_CH // LANES):
            _put_rows(g2, lb, 0, _get_rows(g2, lb, TM_MIX, HALO))

    return kern


def _pool_conv_mixer(x2, t_len, gain, win, wout, pool_w, pool_b, pool_scale, conv_w, conv_b,
                     ln_g, ln_b, next_weights):
    rows = HALO + TM_MIX
    pw = pool_w.astype(BF16)
    z = jnp.zeros((POOL_GC, POOL_GC), BF16)
    pw_bd = jnp.stack([
        jnp.block([[pw[0], z], [z, pw[1]]]),
        jnp.block([[pw[2], z], [z, pw[3]]]),
    ])
    row = lambda n: _const_spec((1, n))
    in_specs = [_tile_spec(TM_MIX), row(D_MODEL), _const_spec(win.shape), _const_spec(pw_bd.shape),
                row(POOL_CH), row(POOL_CH), _const_spec((CONV_WIDTH, CONV_CH)), row(CONV_CH),
                row(CONV_CH), row(CONV_CH), _const_spec(wout.shape)]
    args = [x2, gain.reshape(1, D_MODEL), win, pw_bd, pool_b.reshape(1, POOL_CH),
            pool_scale.reshape(1, POOL_CH), conv_w, conv_b.reshape(1, CONV_CH),
            ln_g.reshape(1, CONV_CH), ln_b.reshape(1, CONV_CH), wout]
    scratch = [pltpu.VMEM((rows, POOL_CH), F32),
               pltpu.VMEM((rows, POOL_CH), F32),
               pltpu.VMEM((rows, POOL_CH - POOL_GC), F32),
               pltpu.VMEM((rows, POOL_CH - 2 * POOL_GC), F32),
               pltpu.VMEM((CONV_CH // LANES, 2 * rows, LANES), F32),
               pltpu.VMEM((TM_MIX, CONV_CH), F32)]
    return _call(_make_pool_conv_kernel(t_len // TM_MIX, len(next_weights)), "pool_conv_mixer",
                 TM_MIX, in_specs, args, next_weights, scratch)


def _make_sgu_kernel(n_cast):
    def kern(*refs):
        ins, srcs, o_ref, dsts, (u_ref, v_ref, gated_ref) = _split_refs(refs, 8, n_cast)
        x_ref, g_ref, win_ref, lg_ref, lb_ref, ws_ref, bs_ref, wout_ref = ins
        n_chunks = TM_SGU // CHUNK
        n_cols = SGU_CH // MXU_TILE
        _cast_step(pl.program_id(0), srcs, dsts)

        x = x_ref[...]
        xn = _rms(x, g_ref[...]).astype(BF16)
        zv = []
        for c in range(n_cols):
            w_cols = slice(SGU_CH + c * MXU_TILE, SGU_CH + (c + 1) * MXU_TILE)
            zv.append(_gelu(jnp.dot(xn, win_ref[:, w_cols], preferred_element_type=F32)))
        heads_per_tile = MXU_TILE // SGU_HC
        for c in range(n_cols):
            cols = slice(c * MXU_TILE, (c + 1) * MXU_TILE)
            uc = _gelu(jnp.dot(xn, win_ref[:, cols], preferred_element_type=F32))
            for k in range(heads_per_tile):
                u_ref[c * heads_per_tile + k] = uc[:, k * SGU_HC:(k + 1) * SGU_HC]
        v = _layer_norm(jnp.concatenate(zv, axis=-1), lg_ref[...], lb_ref[...]).astype(BF16)
        for hd in range(SGU_HEADS):
            v_ref[hd] = v[:, hd * SGU_HC:(hd + 1) * SGU_HC]

        row = lax.broadcasted_iota(jnp.int32, (CHUNK, CHUNK), 0)
        col = lax.broadcasted_iota(jnp.int32, (CHUNK, CHUNK), 1)
        mask = (col <= row).astype(F32)
        for hd in range(SGU_HEADS):
            lanes = slice(hd * SGU_HC, (hd + 1) * SGU_HC)
            w = (ws_ref[hd] * mask).astype(BF16)
            rhs = jnp.concatenate(
                [v_ref[hd, n * CHUNK:(n + 1) * CHUNK, :] for n in range(n_chunks)], axis=-1)
            r = jnp.dot(w, rhs, preferred_element_type=F32)
            for n in range(n_chunks):
                rows_n = slice(n * CHUNK, (n + 1) * CHUNK)
                vo = r[:, n * SGU_HC:(n + 1) * SGU_HC] + bs_ref[:, lanes]
                gated_ref[rows_n, lanes] = (u_ref[hd, rows_n, :] * vo).astype(BF16)
        o_ref[...] = x + jnp.dot(gated_ref[...], wout_ref[...], preferred_element_type=F32)

    return kern


def _sgu_mixer(x2, gain, win, wout, ln_g, ln_b, w_s, b_s, next_weights):
    bias = jnp.repeat(b_s.T, SGU_HC, axis=1)
    in_specs = [_tile_spec(TM_SGU), _const_spec((1, D_MODEL)), _const_spec(win.shape),
                _const_spec((1, SGU_CH)), _const_spec((1, SGU_CH)),
                _const_spec((SGU_HEADS, CHUNK, CHUNK)), _const_spec((CHUNK, SGU_CH)),
                _const_spec(wout.shape)]
    args = [x2, gain.reshape(1, D_MODEL), win, ln_g.reshape(1, SGU_CH), ln_b.reshape(1, SGU_CH),
            w_s, bias, wout]
    scratch = [pltpu.VMEM((SGU_HEADS, TM_SGU, SGU_HC), F32),
               pltpu.VMEM((SGU_HEADS, TM_SGU, SGU_HC), BF16),
               pltpu.VMEM((TM_SGU, SGU_CH), BF16)]
    return _call(_make_sgu_kernel(len(next_weights)), "sgu_mixer", TM_SGU, in_specs, args,
                 next_weights, scratch)


def kernel(x, ffn1_norm, ffn1_w_in, ffn1_w_out, mix_norm, ffn2_norm, ffn2_w_in, ffn2_w_out,
           ab_w_in, pool_w, pool_b, pool_scale, conv_w, conv_b, conv_ln_g, conv_ln_b, ab_w_out,
           sgu_w_in, sgu_ln_g, sgu_ln_b, sgu_w, sgu_b, sgu_w_out, final_norm):
    bsz, t_len, d = x.shape
    depth = ffn1_norm.shape[0]
    assert d == D_MODEL and t_len % max(TM_MIX, TM_SGU, TM_FFN) == 0
    assert (bsz * t_len) // max(TM_MIX, TM_SGU, TM_FFN) >= N_CAST_CHUNKS
    x2 = x.reshape(bsz * t_len, d)
    w = _cast_bf16([(ffn1_w_in, 0), (ffn1_w_out, 0)])
    for i in range(depth):
        j = i // 2
        pool_layer = i % 2 == 0
        mixer_weights = [(ab_w_in, j), (ab_w_out, j)] if pool_layer else [(sgu_w_in, j), (sgu_w_out, j)]
        x2, w = _ffn(x2, ffn1_norm[i], w[0], w[1], mixer_weights + [(ffn2_w_in, i), (ffn2_w_out, i)])
        w_mix, w_ffn2 = w[:2], w[2:]
        if pool_layer:
            x2, _ = _pool_conv_mixer(x2, t_len, mix_norm[i], w_mix[0], w_mix[1], pool_w[j], pool_b[j],
                                     pool_scale[j], conv_w[j], conv_b[j], conv_ln_g[j], conv_ln_b[j], [])
        else:
            x2, _ = _sgu_mixer(x2, mix_norm[i], w_mix[0], w_mix[1], sgu_ln_g[j], sgu_ln_b[j], sgu_w[j],
                               sgu_b[j], [])
        last = i == depth - 1
        next_ffn1 = [] if last else [(ffn1_w_in, i + 1), (ffn1_w_out, i + 1)]
        x2, w = _ffn(x2, ffn2_norm[i], w_ffn2[0], w_ffn2[1], next_ffn1, final_norm if last else None)
    return x2.reshape(bsz, t_len, d)
```

```python
import jax
import jax.numpy as jnp
from jax import lax
from jax.experimental import pallas as pl
from jax.experimental.pallas import tpu as pltpu

D_MODEL = 1024
D_FF = 2816
POOL_CH = 512
POOL_WINDOWS = (2, 4, 8, 16)
POOL_GC = 128
CONV_CH = 512
CONV_WIDTH = 31
SGU_CH = 1024
SGU_HEADS = 8
SGU_HC = 128
CHUNK = 128
EPS = 1e-6

MXU_TILE = 256
FF_CHUNK = MXU_TILE
N_FF_CHUNKS = D_FF // FF_CHUNK

TM_FFN = 1024
TM_MIX = 1024
TM_SGU = 1024

HALO = 32
CONV_RB = 64

N_CAST_CHUNKS = 16
N_CAST_CHUNKS_ALONE = 4
LANES = 128
SUBLANES = 8
BF16_SUBLANES = 16

VMEM_LIMIT_BYTES = 56 * 1024 * 1024

F32 = jnp.float32
BF16 = jnp.bfloat16


def _rms(x, g):
    return x * lax.rsqrt(jnp.mean(x * x, axis=-1, keepdims=True) + EPS) * g


def _norm_parts(x, g):
    rinv = lax.rsqrt(jnp.mean(x * x, axis=-1, keepdims=True) + EPS)
    return (x * g).astype(BF16), rinv


def _layer_norm(x, g, b):
    mu = jnp.mean(x, axis=-1, keepdims=True)
    xc = x - mu
    var = jnp.mean(xc * xc, axis=-1, keepdims=True)
    return xc * lax.rsqrt(var + EPS) * g + b


def _sublane_groups(a):
    rows, lanes = a.shape
    return a.reshape(rows // SUBLANES, SUBLANES, lanes)


def _rows_earlier(a, d):
    a3 = _sublane_groups(a)
    rolled = pltpu.roll(a3, d, axis=1)
    sub = lax.broadcasted_iota(jnp.int32, (1, SUBLANES, 1), 1)
    out = jnp.where(sub >= d, rolled[1:], rolled[:-1])
    return out.reshape(a.shape[0] - SUBLANES, a.shape[1])


def _put_rows(buf, slab, start, value):
    buf[slab, pl.ds(2 * start, value.shape[0], stride=2), :] = value


def _get_rows(buf, slab, start, n):
    return buf[slab, pl.ds(2 * start, n, stride=2), :]


def _gelu(x):
    return 0.5 * x * (1.0 + lax.erf(x * (2.0 ** -0.5)))


def _const_spec(shape):
    nd = len(shape)
    return pl.BlockSpec(shape, lambda *_: (0,) * nd, pipeline_mode=pl.Buffered(1))


def _tile_spec(tm):
    return pl.BlockSpec((tm, D_MODEL), lambda i: (i, 0))


def _cast_plan(w_stack, layer, n_chunks=N_CAST_CHUNKS):
    _, r, c = w_stack.shape
    assert r % (n_chunks * BF16_SUBLANES) == 0
    rows = r // n_chunks
    chunk = lambda i: jnp.minimum(i, n_chunks - 1)
    src = pl.BlockSpec((None, rows, c), lambda i: (layer, chunk(i), 0))
    dst = pl.BlockSpec((rows, c), lambda i: (chunk(i), 0))
    return src, dst, jax.ShapeDtypeStruct((r, c), BF16)


def _cast_step(step, srcs, dsts):
    @pl.when(step < N_CAST_CHUNKS)
    def _():
        for s, d in zip(srcs, dsts):
            d[...] = s[...].astype(BF16)


def _cast_kernel(*refs):
    srcs, dsts = refs[:len(refs) // 2], refs[len(refs) // 2:]
    for s, d in zip(srcs, dsts):
        d[...] = s[...].astype(BF16)


def _cast_bf16(weights):
    plans = [_cast_plan(w, layer, N_CAST_CHUNKS_ALONE) for w, layer in weights]
    return pl.pallas_call(
        _cast_kernel, grid=(N_CAST_CHUNKS_ALONE,), in_specs=[p[0] for p in plans],
        out_specs=[p[1] for p in plans], out_shape=[p[2] for p in plans],
        compiler_params=pltpu.CompilerParams(
            dimension_semantics=("arbitrary",), vmem_limit_bytes=VMEM_LIMIT_BYTES),
        name="cast_bf16")(*[w for w, _ in weights])


def _split_refs(refs, n_in, n_cast):
    bounds = [0, n_in, n_in + n_cast, n_in + n_cast + 1, n_in + 2 * n_cast + 1, len(refs)]
    ins, srcs, (o_ref,), dsts, scratch = [refs[lo:hi] for lo, hi in zip(bounds[:-1], bounds[1:])]
    return ins, srcs, o_ref, dsts, scratch


def _call(kern, name, tm, in_specs, args, next_weights, scratch_shapes):
    n_tok = args[0].shape[0]
    plans = [_cast_plan(w, layer) for w, layer in next_weights]
    outs = pl.pallas_call(
        kern,
        grid=(n_tok // tm,),
        in_specs=in_specs + [p[0] for p in plans],
        out_specs=[_tile_spec(tm)] + [p[1] for p in plans],
        out_shape=[jax.ShapeDtypeStruct((n_tok, D_MODEL), F32)] + [p[2] for p in plans],
        scratch_shapes=scratch_shapes,
        compiler_params=pltpu.CompilerParams(
            dimension_semantics=("arbitrary",), vmem_limit_bytes=VMEM_LIMIT_BYTES),
        name=name,
    )(*args, *[w for w, _ in next_weights])
    return outs[0], outs[1:]


def _make_ffn_kernel(final, n_cast):
    n_in = 5 if final else 4

    def kern(*refs):
        ins, srcs, o_ref, dsts, (a_ref,) = _split_refs(refs, n_in, n_cast)
        x_ref, g_ref, win_ref, wout_ref = ins[:4]
        _cast_step(pl.program_id(0), srcs, dsts)
        x = x_ref[...]
        xg, rinv = _norm_parts(x, g_ref[...])
        for c in range(N_FF_CHUNKS):
            cols = slice(c * FF_CHUNK, (c + 1) * FF_CHUNK)
            up_cols = slice(D_FF + c * FF_CHUNK, D_FF + (c + 1) * FF_CHUNK)
            gate = jnp.dot(xg, win_ref[:, cols], preferred_element_type=F32) * rinv
            up = jnp.dot(xg, win_ref[:, up_cols], preferred_element_type=F32) * rinv
            a_ref[:, cols] = (jax.nn.silu(gate) * up).astype(BF16)
        y = jnp.dot(a_ref[...], wout_ref[...], preferred_element_type=F32)
        out = x + 0.5 * y
        if final:
            out = _rms(out, ins[4][...])
        o_ref[...] = out

    return kern


def _ffn(x2, gain, win, wout, next_weights, final_gain=None):
    in_specs = [_tile_spec(TM_FFN), _const_spec((1, D_MODEL)), _const_spec(win.shape),
                _const_spec(wout.shape)]
    args = [x2, gain.reshape(1, D_MODEL), win, wout]
    final = final_gain is not None
    if final:
        in_specs.append(_const_spec((1, D_MODEL)))
        args.append(final_gain.reshape(1, D_MODEL))
    return _call(_make_ffn_kernel(final, len(next_weights)), "ffn_final" if final else "ffn",
                 TM_FFN, in_specs, args, next_weights, [pltpu.VMEM((TM_FFN, D_FF), BF16)])


def _make_pool_conv_kernel(tiles_per_seq, n_cast):
    def kern(*refs):
        ins, srcs, o_ref, dsts, scratch = _split_refs(refs, 11, n_cast)
        (x_ref, g_ref, win_ref, pw_ref, pb_ref, ps_ref, cw_ref, cb_ref,
         lg_ref, lb_ref, wout_ref) = ins
        ubuf, s2buf, s4buf, s8buf, g2, ybuf = scratch
        step = pl.program_id(0)
        t = step % tiles_per_seq
        rows = HALO + TM_MIX
        _cast_step(step, srcs, dsts)

        @pl.when(t == 0)
        def _():
            ubuf[0:HALO, :] = jnp.zeros((HALO, POOL_CH), F32)
            for lb in range(CONV_CH // LANES):
                _put_rows(g2, lb, 0, jnp.zeros((HALO, LANES), F32))

        x = x_ref[...]
        xn = _rms(x, g_ref[...]).astype(BF16)
        gate = jnp.dot(xn, win_ref[:, POOL_CH + CONV_CH:], preferred_element_type=F32)
        sig = jax.nn.sigmoid(gate)
        u = jnp.dot(xn, win_ref[:, :POOL_CH], preferred_element_type=F32)
        ubuf[HALO:, :] = u
        glu = jnp.dot(xn, win_ref[:, POOL_CH:POOL_CH + CONV_CH], preferred_element_type=F32) * sig
        for lb in range(CONV_CH // LANES):
            _put_rows(g2, lb, HALO, glu[:, lb * LANES:(lb + 1) * LANES])

        g1 = POOL_GC
        s2buf[8:, :] = ubuf[8:, :] + _rows_earlier(ubuf[...], 1)
        s4buf[16:, :] = s2buf[16:, g1:] + _rows_earlier(s2buf[8:, g1:], 2)
        s8buf[24:, :] = s4buf[24:, g1:] + _rows_earlier(s4buf[16:, g1:], 4)
        s16 = s8buf[HALO:, g1:] + s8buf[pl.ds(HALO - 8, TM_MIX), g1:]
        sums = [s2buf[HALO:, :g1], s4buf[HALO:, :g1], s8buf[HALO:, :g1], s16]
        pos = t * TM_MIX + lax.broadcasted_iota(jnp.int32, (TM_MIX, 1), 0)
        pooled = []
        for gi, win in enumerate(POOL_WINDOWS):
            count = jnp.minimum(pos + 1, win).astype(F32)
            pooled.append(sums[gi] / count - u[:, gi * POOL_GC:(gi + 1) * POOL_GC])
        ya = []
        for p in range(2):
            pp = jnp.concatenate(pooled[2 * p:2 * p + 2], axis=-1).astype(BF16)
            ya.append(jnp.dot(pp, pw_ref[p], preferred_element_type=F32))
        ya = (jnp.concatenate(ya, axis=-1) + pb_ref[...]) * ps_ref[...]

        taps_by_shift = [[] for _ in range(8)]
        for k in range(CONV_WIDTH):
            q, r = divmod(HALO - (CONV_WIDTH - 1) + k, 8)
            taps_by_shift[r].append((k, q))

        def conv_rows(rb, carry):
            base = pl.multiple_of(rb * CONV_RB, CONV_RB)
            for lb in range(CONV_CH // LANES):
                lanes = slice(lb * LANES, (lb + 1) * LANES)
                acc = None
                for r, taps in enumerate(taps_by_shift):
                    span = CONV_RB + 8 * max(q for _, q in taps)
                    slab = _get_rows(g2, lb, base + r, span)
                    for k, q in taps:
                        term = slab[8 * q:8 * q + CONV_RB] * cw_ref[k:k + 1, lanes]
                        acc = term if acc is None else acc + term
                ybuf[pl.ds(base, CONV_RB), lanes] = acc
            return carry

        lax.fori_loop(0, TM_MIX // CONV_RB, conv_rows, 0, unroll=4)
        y = ybuf[...] + cb_ref[...]
        yb = jax.nn.silu(_layer_norm(y, lg_ref[...], lb_ref[...]))

        cat = jnp.concatenate([ya.astype(BF16), yb.astype(BF16)], axis=-1)
        o_ref[...] = x + jnp.dot(cat, wout_ref[...], preferred_element_type=F32)

        ubuf[0:HALO, :] = ubuf[TM_MIX:, :]
        for lb in range(CONV_CH // LANES):
            _put_rows(g2, lb, 0, _get_rows(g2, lb, TM_MIX, HALO))

    return kern


def _pool_conv_mixer(x2, t_len, gain, win, wout, pool_w, pool_b, pool_scale, conv_w, conv_b,
                     ln_g, ln_b, next_weights):
    rows = HALO + TM_MIX
    pw = pool_w.astype(BF16)
    z = jnp.zeros((POOL_GC, POOL_GC), BF16)
    pw_bd = jnp.stack([
        jnp.block([[pw[0], z], [z, pw[1]]]),
        jnp.block([[pw[2], z], [z, pw[3]]]),
    ])
    row = lambda n: _const_spec((1, n))
    in_specs = [_tile_spec(TM_MIX), row(D_MODEL), _const_spec(win.shape), _const_spec(pw_bd.shape),
                row(POOL_CH), row(POOL_CH), _const_spec((CONV_WIDTH, CONV_CH)), row(CONV_CH),
                row(CONV_CH), row(CONV_CH), _const_spec(wout.shape)]
    args = [x2, gain.reshape(1, D_MODEL), win, pw_bd, pool_b.reshape(1, POOL_CH),
            pool_scale.reshape(1, POOL_CH), conv_w, conv_b.reshape(1, CONV_CH),
            ln_g.reshape(1, CONV_CH), ln_b.reshape(1, CONV_CH), wout]
    scratch = [pltpu.VMEM((rows, POOL_CH), F32),
               pltpu.VMEM((rows, POOL_CH), F32),
               pltpu.VMEM((rows, POOL_CH - POOL_GC), F32),
               pltpu.VMEM((rows, POOL_CH - 2 * POOL_GC), F32),
               pltpu.VMEM((CONV_CH // LANES, 2 * rows, LANES), F32),
               pltpu.VMEM((TM_MIX, CONV_CH), F32)]
    return _call(_make_pool_conv_kernel(t_len // TM_MIX, len(next_weights)), "pool_conv_mixer",
                 TM_MIX, in_specs, args, next_weights, scratch)


def _make_sgu_kernel(n_cast):
    def kern(*refs):
        ins, srcs, o_ref, dsts, (u_ref, v_ref, gated_ref) = _split_refs(refs, 8, n_cast)
        x_ref, g_ref, win_ref, lg_ref, lb_ref, ws_ref, bs_ref, wout_ref = ins
        n_chunks = TM_SGU // CHUNK
        n_cols = SGU_CH // MXU_TILE
        _cast_step(pl.program_id(0), srcs, dsts)

        x = x_ref[...]
        xn = _rms(x, g_ref[...]).astype(BF16)
        zv = []
        for c in range(n_cols):
            w_cols = slice(SGU_CH + c * MXU_TILE, SGU_CH + (c + 1) * MXU_TILE)
            zv.append(_gelu(jnp.dot(xn, win_ref[:, w_cols], preferred_element_type=F32)))
        heads_per_tile = MXU_TILE // SGU_HC
        for c in range(n_cols):
            cols = slice(c * MXU_TILE, (c + 1) * MXU_TILE)
            uc = _gelu(jnp.dot(xn, win_ref[:, cols], preferred_element_type=F32))
            for k in range(heads_per_tile):
                u_ref[c * heads_per_tile + k] = uc[:, k * SGU_HC:(k + 1) * SGU_HC]
        v = _layer_norm(jnp.concatenate(zv, axis=-1), lg_ref[...], lb_ref[...]).astype(BF16)
        for hd in range(SGU_HEADS):
            v_ref[hd] = v[:, hd * SGU_HC:(hd + 1) * SGU_HC]

        row = lax.broadcasted_iota(jnp.int32, (CHUNK, CHUNK), 0)
        col = lax.broadcasted_iota(jnp.int32, (CHUNK, CHUNK), 1)
        mask = (col <= row).astype(F32)
        for hd in range(SGU_HEADS):
            lanes = slice(hd * SGU_HC, (hd + 1) * SGU_HC)
            w = (ws_ref[hd] * mask).astype(BF16)
            rhs = jnp.concatenate(
                [v_ref[hd, n * CHUNK:(n + 1) * CHUNK, :] for n in range(n_chunks)], axis=-1)
            r = jnp.dot(w, rhs, preferred_element_type=F32)
            for n in range(n_chunks):
                rows_n = slice(n * CHUNK, (n + 1) * CHUNK)
                vo = r[:, n * SGU_HC:(n + 1) * SGU_HC] + bs_ref[:, lanes]
                gated_ref[rows_n, lanes] = (u_ref[hd, rows_n, :] * vo).astype(BF16)
        o_ref[...] = x + jnp.dot(gated_ref[...], wout_ref[...], preferred_element_type=F32)

    return kern


def _sgu_mixer(x2, gain, win, wout, ln_g, ln_b, w_s, b_s, next_weights):
    bias = jnp.repeat(b_s.T, SGU_HC, axis=1)
    in_specs = [_tile_spec(TM_SGU), _const_spec((1, D_MODEL)), _const_spec(win.shape),
                _const_spec((1, SGU_CH)), _const_spec((1, SGU_CH)),
                _const_spec((SGU_HEADS, CHUNK, CHUNK)), _const_spec((CHUNK, SGU_CH)),
                _const_spec(wout.shape)]
    args = [x2, gain.reshape(1, D_MODEL), win, ln_g.reshape(1, SGU_CH), ln_b.reshape(1, SGU_CH),
            w_s, bias, wout]
    scratch = [pltpu.VMEM((SGU_HEADS, TM_SGU, SGU_HC), F32),
               pltpu.VMEM((SGU_HEADS, TM_SGU, SGU_HC), BF16),
               pltpu.VMEM((TM_SGU, SGU_CH), BF16)]
    return _call(_make_sgu_kernel(len(next_weights)), "sgu_mixer", TM_SGU, in_specs, args,
                 next_weights, scratch)


def kernel(x, ffn1_norm, ffn1_w_in, ffn1_w_out, mix_norm, ffn2_norm, ffn2_w_in, ffn2_w_out,
           ab_w_in, pool_w, pool_b, pool_scale, conv_w, conv_b, conv_ln_g, conv_ln_b, ab_w_out,
           sgu_w_in, sgu_ln_g, sgu_ln_b, sgu_w, sgu_b, sgu_w_out, final_norm):
    bsz, t_len, d = x.shape
    depth = ffn1_norm.shape[0]
    assert d == D_MODEL and t_len % max(TM_MIX, TM_SGU, TM_FFN) == 0
    assert (bsz * t_len) // max(TM_MIX, TM_SGU, TM_FFN) >= N_CAST_CHUNKS
    x2 = x.reshape(bsz * t_len, d)
    w = _cast_bf16([(ffn1_w_in, 0), (ffn1_w_out, 0)])
    for i in range(depth):
        j = i // 2
        pool_layer = i % 2 == 0
        mixer_weights = [(ab_w_in, j), (ab_w_out, j)] if pool_layer else [(sgu_w_in, j), (sgu_w_out, j)]
        x2, w = _ffn(x2, ffn1_norm[i], w[0], w[1], mixer_weights + [(ffn2_w_in, i), (ffn2_w_out, i)])
        w_mix, w_ffn2 = w[:2], w[2:]
        if pool_layer:
            x2, _ = _pool_conv_mixer(x2, t_len, mix_norm[i], w_mix[0], w_mix[1], pool_w[j], pool_b[j],
                                     pool_scale[j], conv_w[j], conv_b[j], conv_ln_g[j], conv_ln_b[j], [])
        else:
            x2, _ = _sgu_mixer(x2, mix_norm[i], w_mix[0], w_mix[1], sgu_ln_g[j], sgu_ln_b[j], sgu_w[j],
                               sgu_b[j], [])
        last = i == depth - 1
        next_ffn1 = [] if last else [(ffn1_w_in, i + 1), (ffn1_w_out, i + 1)]
        x2, w = _ffn(x2, ffn2_norm[i], w_ffn2[0], w_ffn2[1], next_ffn1, final_norm if last else None)
    return x2.reshape(bsz, t_len, d)
```

```python
import jax
import jax.numpy as jnp
from jax import lax
from jax.experimental import pallas as pl
from jax.experimental.pallas import tpu as pltpu

D_MODEL = 1024
D_FF = 2816
POOL_CH = 512
POOL_WINDOWS = (2, 4, 8, 16)
POOL_GC = 128
CONV_CH = 512
CONV_WIDTH = 31
SGU_CH = 1024
SGU_HEADS = 8
SGU_HC = 128
CHUNK = 128
EPS = 1e-6

MXU_TILE = 256
FF_CHUNK = MXU_TILE
N_FF_CHUNKS = D_FF // FF_CHUNK

TM_FFN = 1024
TM_MIX = 1024
TM_SGU = 1024

HALO = 32
CONV_RB = 64

N_CAST_CHUNKS = 16
N_CAST_CHUNKS_ALONE = 4
LANES = 128
SUBLANES = 8
BF16_SUBLANES = 16

VMEM_LIMIT_BYTES = 56 * 1024 * 1024

F32 = jnp.float32
BF16 = jnp.bfloat16


def _rms(x, g):
    return x * lax.rsqrt(jnp.mean(x * x, axis=-1, keepdims=True) + EPS) * g


def _norm_parts(x, g):
    rinv = lax.rsqrt(jnp.mean(x * x, axis=-1, keepdims=True) + EPS)
    return (x * g).astype(BF16), rinv


def _layer_norm(x, g, b):
    mu = jnp.mean(x, axis=-1, keepdims=True)
    xc = x - mu
    var = jnp.mean(xc * xc, axis=-1, keepdims=True)
    return xc * lax.rsqrt(var + EPS) * g + b


def _sublane_groups(a):
    rows, lanes = a.shape
    return a.reshape(rows // SUBLANES, SUBLANES, lanes)


def _rows_earlier(a, d):
    a3 = _sublane_groups(a)
    rolled = pltpu.roll(a3, d, axis=1)
    sub = lax.broadcasted_iota(jnp.int32, (1, SUBLANES, 1), 1)
    out = jnp.where(sub >= d, rolled[1:], rolled[:-1])
    return out.reshape(a.shape[0] - SUBLANES, a.shape[1])


def _put_rows(buf, slab, start, value):
    buf[slab, pl.ds(2 * start, value.shape[0], stride=2), :] = value


def _get_rows(buf, slab, start, n):
    return buf[slab, pl.ds(2 * start, n, stride=2), :]


def _gelu(x):
    return 0.5 * x * (1.0 + lax.erf(x * (2.0 ** -0.5)))


def _const_spec(shape):
    nd = len(shape)
    return pl.BlockSpec(shape, lambda *_: (0,) * nd, pipeline_mode=pl.Buffered(1))


def _tile_spec(tm):
    return pl.BlockSpec((tm, D_MODEL), lambda i: (i, 0))


def _cast_plan(w_stack, layer, n_chunks=N_CAST_CHUNKS):
    _, r, c = w_stack.shape
    assert r % (n_chunks * BF16_SUBLANES) == 0
    rows = r // n_chunks
    chunk = lambda i: jnp.minimum(i, n_chunks - 1)
    src = pl.BlockSpec((None, rows, c), lambda i: (layer, chunk(i), 0))
    dst = pl.BlockSpec((rows, c), lambda i: (chunk(i), 0))
    return src, dst, jax.ShapeDtypeStruct((r, c), BF16)


def _cast_step(step, srcs, dsts):
    @pl.when(step < N_CAST_CHUNKS)
    def _():
        for s, d in zip(srcs, dsts):
            d[...] = s[...].astype(BF16)


def _cast_kernel(*refs):
    srcs, dsts = refs[:len(refs) // 2], refs[len(refs) // 2:]
    for s, d in zip(srcs, dsts):
        d[...] = s[...].astype(BF16)


def _cast_bf16(weights):
    plans = [_cast_plan(w, layer, N_CAST_CHUNKS_ALONE) for w, layer in weights]
    return pl.pallas_call(
        _cast_kernel, grid=(N_CAST_CHUNKS_ALONE,), in_specs=[p[0] for p in plans],
        out_specs=[p[1] for p in plans], out_shape=[p[2] for p in plans],
        compiler_params=pltpu.CompilerParams(
            dimension_semantics=("arbitrary",), vmem_limit_bytes=VMEM_LIMIT_BYTES),
        name="cast_bf16")(*[w for w, _ in weights])


def _split_refs(refs, n_in, n_cast):
    bounds = [0, n_in, n_in + n_cast, n_in + n_cast + 1, n_in + 2 * n_cast + 1, len(refs)]
    ins, srcs, (o_ref,), dsts, scratch = [refs[lo:hi] for lo, hi in zip(bounds[:-1], bounds[1:])]
    return ins, srcs, o_ref, dsts, scratch


def _call(kern, name, tm, in_specs, args, next_weights, scratch_shapes):
    n_tok = args[0].shape[0]
    plans = [_cast_plan(w, layer) for w, layer in next_weights]
    outs = pl.pallas_call(
        kern,
        grid=(n_tok // tm,),
        in_specs=in_specs + [p[0] for p in plans],
        out_specs=[_tile_spec(tm)] + [p[1] for p in plans],
        out_shape=[jax.ShapeDtypeStruct((n_tok, D_MODEL), F32)] + [p[2] for p in plans],
        scratch_shapes=scratch_shapes,
        compiler_params=pltpu.CompilerParams(
            dimension_semantics=("arbitrary",), vmem_limit_bytes=VMEM_LIMIT_BYTES),
        name=name,
    )(*args, *[w for w, _ in next_weights])
    return outs[0], outs[1:]


def _make_ffn_kernel(final, n_cast):
    n_in = 5 if final else 4

    def kern(*refs):
        ins, srcs, o_ref, dsts, (a_ref, wout_ref, wout_sem) = _split_refs(refs, n_in, n_cast)
        x_ref, g_ref, win_ref, wout_hbm = ins[:4]
        step = pl.program_id(0)
        _cast_step(step, srcs, dsts)
        wout_copy = pltpu.make_async_copy(wout_hbm, wout_ref, wout_sem)

        def body(first_step):
            if first_step:
                wout_copy.start()
            x = x_ref[...]
            xg, rinv = _norm_parts(x, g_ref[...])
            for c in range(N_FF_CHUNKS):
                cols = slice(c * FF_CHUNK, (c + 1) * FF_CHUNK)
                up_cols = slice(D_FF + c * FF_CHUNK, D_FF + (c + 1) * FF_CHUNK)
                gate = jnp.dot(xg, win_ref[:, cols], preferred_element_type=F32) * rinv
                up = jnp.dot(xg, win_ref[:, up_cols], preferred_element_type=F32) * rinv
                a_ref[:, cols] = (jax.nn.silu(gate) * up).astype(BF16)
            if first_step:
                wout_copy.wait()
            y = jnp.dot(a_ref[...], wout_ref[...], preferred_element_type=F32)
            out = x + 0.5 * y
            if final:
                out = _rms(out, ins[4][...])
            o_ref[...] = out

        @pl.when(step == 0)
        def _():
            body(True)

        @pl.when(step > 0)
        def _():
            body(False)

    return kern


def _ffn(x2, gain, win, wout, next_weights, final_gain=None):
    in_specs = [_tile_spec(TM_FFN), _const_spec((1, D_MODEL)), _const_spec(win.shape),
                pl.BlockSpec(memory_space=pl.ANY)]
    args = [x2, gain.reshape(1, D_MODEL), win, wout]
    final = final_gain is not None
    if final:
        in_specs.append(_const_spec((1, D_MODEL)))
        args.append(final_gain.reshape(1, D_MODEL))
    scratch = [pltpu.VMEM((TM_FFN, D_FF), BF16),
               pltpu.VMEM(wout.shape, BF16),
               pltpu.SemaphoreType.DMA(())]
    return _call(_make_ffn_kernel(final, len(next_weights)), "ffn_final" if final else "ffn",
                 TM_FFN, in_specs, args, next_weights, scratch)


def _make_pool_conv_kernel(tiles_per_seq, n_cast):
    def kern(*refs):
        ins, srcs, o_ref, dsts, scratch = _split_refs(refs, 11, n_cast)
        (x_ref, g_ref, win_ref, pw_ref, pb_ref, ps_ref, cw_ref, cb_ref,
         lg_ref, lb_ref, wout_ref) = ins
        ubuf, s2buf, s4buf, s8buf, g2, ybuf = scratch
        step = pl.program_id(0)
        t = step % tiles_per_seq
        rows = HALO + TM_MIX
        _cast_step(step, srcs, dsts)

        @pl.when(t == 0)
        def _():
            ubuf[0:HALO, :] = jnp.zeros((HALO, POOL_CH), F32)
            for lb in range(CONV_CH // LANES):
                _put_rows(g2, lb, 0, jnp.zeros((HALO, LANES), F32))

        x = x_ref[...]
        xn = _rms(x, g_ref[...]).astype(BF16)
        gate = jnp.dot(xn, win_ref[:, POOL_CH + CONV_CH:], preferred_element_type=F32)
        sig = jax.nn.sigmoid(gate)
        u = jnp.dot(xn, win_ref[:, :POOL_CH], preferred_element_type=F32)
        ubuf[HALO:, :] = u
        glu = jnp.dot(xn, win_ref[:, POOL_CH:POOL_CH + CONV_CH], preferred_element_type=F32) * sig
        for lb in range(CONV_CH // LANES):
            _put_rows(g2, lb, HALO, glu[:, lb * LANES:(lb + 1) * LANES])

        g1 = POOL_GC
        s2buf[8:, :] = ubuf[8:, :] + _rows_earlier(ubuf[...], 1)
        s4buf[16:, :] = s2buf[16:, g1:] + _rows_earlier(s2buf[8:, g1:], 2)
        s8buf[24:, :] = s4buf[24:, g1:] + _rows_earlier(s4buf[16:, g1:], 4)
        s16 = s8buf[HALO:, g1:] + s8buf[pl.ds(HALO - 8, TM_MIX), g1:]
        sums = [s2buf[HALO:, :g1], s4buf[HALO:, :g1], s8buf[HALO:, :g1], s16]
        pos = t * TM_MIX + lax.broadcasted_iota(jnp.int32, (TM_MIX, 1), 0)
        pooled = []
        for gi, win in enumerate(POOL_WINDOWS):
            count = jnp.minimum(pos + 1, win).astype(F32)
            pooled.append(sums[gi] / count - u[:, gi * POOL_GC:(gi + 1) * POOL_GC])
        ya = []
        for p in range(2):
            pp = jnp.concatenate(pooled[2 * p:2 * p + 2], axis=-1).astype(BF16)
            ya.append(jnp.dot(pp, pw_ref[p], preferred_element_type=F32))
        ya = (jnp.concatenate(ya, axis=-1) + pb_ref[...]) * ps_ref[...]

        taps_by_shift = [[] for _ in range(8)]
        for k in range(CONV_WIDTH):
            q, r = divmod(HALO - (CONV_WIDTH - 1) + k, 8)
            taps_by_shift[r].append((k, q))

        def conv_rows(rb, carry):
            base = pl.multiple_of(rb * CONV_RB, CONV_RB)
            for lb in range(CONV_CH // LANES):
                lanes = slice(lb * LANES, (lb + 1) * LANES)
                acc = None
                for r, taps in enumerate(taps_by_shift):
                    span = CONV_RB + 8 * max(q for _, q in taps)
                    slab = _get_rows(g2, lb, base + r, span)
                    for k, q in taps:
                        term = slab[8 * q:8 * q + CONV_RB] * cw_ref[k:k + 1, lanes]
                        acc = term if acc is None else acc + term
                ybuf[pl.ds(base, CONV_RB), lanes] = acc
            return carry

        lax.fori_loop(0, TM_MIX // CONV_RB, conv_rows, 0, unroll=4)
        y = ybuf[...] + cb_ref[...]
        yb = jax.nn.silu(_layer_norm(y, lg_ref[...], lb_ref[...]))

        cat = jnp.concatenate([ya.astype(BF16), yb.astype(BF16)], axis=-1)
        o_ref[...] = x + jnp.dot(cat, wout_ref[...], preferred_element_type=F32)

        ubuf[0:HALO, :] = ubuf[TM_MIX:, :]
        for lb in range(CONV_CH // LANES):
            _put_rows(g2, lb, 0, _get_rows(g2, lb, TM_MIX, HALO))

    return kern


def _pool_conv_mixer(x2, t_len, gain, win, wout, pool_w, pool_b, pool_scale, conv_w, conv_b,
                     ln_g, ln_b, next_weights):
    rows = HALO + TM_MIX
    pw = pool_w.astype(BF16)
    z = jnp.zeros((POOL_GC, POOL_GC), BF16)
    pw_bd = jnp.stack([
        jnp.block([[pw[0], z], [z, pw[1]]]),
        jnp.block([[pw[2], z], [z, pw[3]]]),
    ])
    row = lambda n: _const_spec((1, n))
    in_specs = [_tile_spec(TM_MIX), row(D_MODEL), _const_spec(win.shape), _const_spec(pw_bd.shape),
                row(POOL_CH), row(POOL_CH), _const_spec((CONV_WIDTH, CONV_CH)), row(CONV_CH),
                row(CONV_CH), row(CONV_CH), _const_spec(wout.shape)]
    args = [x2, gain.reshape(1, D_MODEL), win, pw_bd, pool_b.reshape(1, POOL_CH),
            pool_scale.reshape(1, POOL_CH), conv_w, conv_b.reshape(1, CONV_CH),
            ln_g.reshape(1, CONV_CH), ln_b.reshape(1, CONV_CH), wout]
    scratch = [pltpu.VMEM((rows, POOL_CH), F32),
               pltpu.VMEM((rows, POOL_CH), F32),
               pltpu.VMEM((rows, POOL_CH - POOL_GC), F32),
               pltpu.VMEM((rows, POOL_CH - 2 * POOL_GC), F32),
               pltpu.VMEM((CONV_CH // LANES, 2 * rows, LANES), F32),
               pltpu.VMEM((TM_MIX, CONV_CH), F32)]
    return _call(_make_pool_conv_kernel(t_len // TM_MIX, len(next_weights)), "pool_conv_mixer",
                 TM_MIX, in_specs, args, next_weights, scratch)


def _make_sgu_kernel(n_cast):
    def kern(*refs):
        ins, srcs, o_ref, dsts, (u_ref, v_ref, gated_ref) = _split_refs(refs, 8, n_cast)
        x_ref, g_ref, win_ref, lg_ref, lb_ref, ws_ref, bs_ref, wout_ref = ins
        n_chunks = TM_SGU // CHUNK
        n_cols = SGU_CH // MXU_TILE
        _cast_step(pl.program_id(0), srcs, dsts)

        x = x_ref[...]
        xn = _rms(x, g_ref[...]).astype(BF16)
        zv = []
        for c in range(n_cols):
            w_cols = slice(SGU_CH + c * MXU_TILE, SGU_CH + (c + 1) * MXU_TILE)
            zv.append(_gelu(jnp.dot(xn, win_ref[:, w_cols], preferred_element_type=F32)))
        heads_per_tile = MXU_TILE // SGU_HC
        for c in range(n_cols):
            cols = slice(c * MXU_TILE, (c + 1) * MXU_TILE)
            uc = _gelu(jnp.dot(xn, win_ref[:, cols], preferred_element_type=F32))
            for k in range(heads_per_tile):
                u_ref[c * heads_per_tile + k] = uc[:, k * SGU_HC:(k + 1) * SGU_HC]
        v = _layer_norm(jnp.concatenate(zv, axis=-1), lg_ref[...], lb_ref[...]).astype(BF16)
        for hd in range(SGU_HEADS):
            v_ref[hd] = v[:, hd * SGU_HC:(hd + 1) * SGU_HC]

        row = lax.broadcasted_iota(jnp.int32, (CHUNK, CHUNK), 0)
        col = lax.broadcasted_iota(jnp.int32, (CHUNK, CHUNK), 1)
        mask = (col <= row).astype(F32)
        for hd in range(SGU_HEADS):
            lanes = slice(hd * SGU_HC, (hd + 1) * SGU_HC)
            w = (ws_ref[hd] * mask).astype(BF16)
            rhs = jnp.concatenate(
                [v_ref[hd, n * CHUNK:(n + 1) * CHUNK, :] for n in range(n_chunks)], axis=-1)
            r = jnp.dot(w, rhs, preferred_element_type=F32)
            for n in range(n_chunks):
                rows_n = slice(n * CHUNK, (n + 1) * CHUNK)
                vo = r[:, n * SGU_HC:(n + 1) * SGU_HC] + bs_ref[:, lanes]
                gated_ref[rows_n, lanes] = (u_ref[hd, rows_n, :] * vo).astype(BF16)
        o_ref[...] = x + jnp.dot(gated_ref[...], wout_ref[...], preferred_element_type=F32)

    return kern


def _sgu_mixer(x2, gain, win, wout, ln_g, ln_b, w_s, b_s, next_weights):
    bias = jnp.repeat(b_s.T, SGU_HC, axis=1)
    in_specs = [_tile_spec(TM_SGU), _const_spec((1, D_MODEL)), _const_spec(win.shape),
                _const_spec((1, SGU_CH)), _const_spec((1, SGU_CH)),
                _const_spec((SGU_HEADS, CHUNK, CHUNK)), _const_spec((CHUNK, SGU_CH)),
                _const_spec(wout.shape)]
    args = [x2, gain.reshape(1, D_MODEL), win, ln_g.reshape(1, SGU_CH), ln_b.reshape(1, SGU_CH),
            w_s, bias, wout]
    scratch = [pltpu.VMEM((SGU_HEADS, TM_SGU, SGU_HC), F32),
               pltpu.VMEM((SGU_HEADS, TM_SGU, SGU_HC), BF16),
               pltpu.VMEM((TM_SGU, SGU_CH), BF16)]
    return _call(_make_sgu_kernel(len(next_weights)), "sgu_mixer", TM_SGU, in_specs, args,
                 next_weights, scratch)


def kernel(x, ffn1_norm, ffn1_w_in, ffn1_w_out, mix_norm, ffn2_norm, ffn2_w_in, ffn2_w_out,
           ab_w_in, pool_w, pool_b, pool_scale, conv_w, conv_b, conv_ln_g, conv_ln_b, ab_w_out,
           sgu_w_in, sgu_ln_g, sgu_ln_b, sgu_w, sgu_b, sgu_w_out, final_norm):
    bsz, t_len, d = x.shape
    depth = ffn1_norm.shape[0]
    assert d == D_MODEL and t_len % max(TM_MIX, TM_SGU, TM_FFN) == 0
    assert (bsz * t_len) // max(TM_MIX, TM_SGU, TM_FFN) >= N_CAST_CHUNKS
    x2 = x.reshape(bsz * t_len, d)
    w = _cast_bf16([(ffn1_w_in, 0), (ffn1_w_out, 0)])
    for i in range(depth):
        j = i // 2
        pool_layer = i % 2 == 0
        mixer_weights = [(ab_w_in, j), (ab_w_out, j)] if pool_layer else [(sgu_w_in, j), (sgu_w_out, j)]
        x2, w = _ffn(x2, ffn1_norm[i], w[0], w[1], mixer_weights + [(ffn2_w_in, i), (ffn2_w_out, i)])
        w_mix, w_ffn2 = w[:2], w[2:]
        if pool_layer:
            x2, _ = _pool_conv_mixer(x2, t_len, mix_norm[i], w_mix[0], w_mix[1], pool_w[j], pool_b[j],
                                     pool_scale[j], conv_w[j], conv_b[j], conv_ln_g[j], conv_ln_b[j], [])
        else:
            x2, _ = _sgu_mixer(x2, mix_norm[i], w_mix[0], w_mix[1], sgu_ln_g[j], sgu_ln_b[j], sgu_w[j],
                               sgu_b[j], [])
        last = i == depth - 1
        next_ffn1 = [] if last else [(ffn1_w_in, i + 1), (ffn1_w_out, i + 1)]
        x2, w = _ffn(x2, ffn2_norm[i], w_ffn2[0], w_ffn2[1], next_ffn1, final_norm if last else None)
    return x2.reshape(bsz, t_len, d)
```

```python
import jax
import jax.numpy as jnp
from jax import lax
from jax.experimental import pallas as pl
from jax.experimental.pallas import tpu as pltpu

D_MODEL = 1024
D_FF = 2816
POOL_CH = 512
POOL_WINDOWS = (2, 4, 8, 16)
POOL_GC = 128
CONV_CH = 512
CONV_WIDTH = 31
SGU_CH = 1024
SGU_HEADS = 8
SGU_HC = 128
CHUNK = 128
EPS = 1e-6

MXU_TILE = 256
FF_CHUNK = MXU_TILE
N_FF_CHUNKS = D_FF // FF_CHUNK

TM_FFN = 1024
TM_MIX = 1024
TM_SGU = 1024

HALO = 32
CONV_RB = 64

N_CAST_CHUNKS = 16
N_CAST_CHUNKS_ALONE = 4
LANES = 128
SUBLANES = 8
BF16_SUBLANES = 16

VMEM_LIMIT_BYTES = 56 * 1024 * 1024

F32 = jnp.float32
BF16 = jnp.bfloat16


def _rms(x, g):
    return x * lax.rsqrt(jnp.mean(x * x, axis=-1, keepdims=True) + EPS) * g


def _norm_parts(x, g):
    rinv = lax.rsqrt(jnp.mean(x * x, axis=-1, keepdims=True) + EPS)
    return (x * g).astype(BF16), rinv


def _layer_norm(x, g, b):
    mu = jnp.mean(x, axis=-1, keepdims=True)
    xc = x - mu
    var = jnp.mean(xc * xc, axis=-1, keepdims=True)
    return xc * lax.rsqrt(var + EPS) * g + b


def _sublane_groups(a):
    rows, lanes = a.shape
    return a.reshape(rows // SUBLANES, SUBLANES, lanes)


def _rows_earlier(a, d):
    a3 = _sublane_groups(a)
    rolled = pltpu.roll(a3, d, axis=1)
    sub = lax.broadcasted_iota(jnp.int32, (1, SUBLANES, 1), 1)
    out = jnp.where(sub >= d, rolled[1:], rolled[:-1])
    return out.reshape(a.shape[0] - SUBLANES, a.shape[1])


def _put_rows(buf, slab, start, value):
    buf[slab, pl.ds(2 * start, value.shape[0], stride=2), :] = value


def _get_rows(buf, slab, start, n):
    return buf[slab, pl.ds(2 * start, n, stride=2), :]


def _gelu(x):
    return 0.5 * x * (1.0 + lax.erf(x * (2.0 ** -0.5)))


def _const_spec(shape):
    nd = len(shape)
    return pl.BlockSpec(shape, lambda *_: (0,) * nd, pipeline_mode=pl.Buffered(1))


def _tile_spec(tm):
    return pl.BlockSpec((tm, D_MODEL), lambda i: (i, 0))


def _cast_plan(w_stack, layer, n_chunks=N_CAST_CHUNKS):
    _, r, c = w_stack.shape
    assert r % (n_chunks * BF16_SUBLANES) == 0
    rows = r // n_chunks
    chunk = lambda i: jnp.minimum(i, n_chunks - 1)
    src = pl.BlockSpec((None, rows, c), lambda i: (layer, chunk(i), 0))
    dst = pl.BlockSpec((rows, c), lambda i: (chunk(i), 0))
    return src, dst, jax.ShapeDtypeStruct((r, c), BF16)


def _cast_step(step, srcs, dsts):
    @pl.when(step < N_CAST_CHUNKS)
    def _():
        for s, d in zip(srcs, dsts):
            d[...] = s[...].astype(BF16)


def _cast_kernel(*refs):
    srcs, dsts = refs[:len(refs) // 2], refs[len(refs) // 2:]
    for s, d in zip(srcs, dsts):
        d[...] = s[...].astype(BF16)


def _cast_bf16(weights):
    plans = [_cast_plan(w, layer, N_CAST_CHUNKS_ALONE) for w, layer in weights]
    return pl.pallas_call(
        _cast_kernel, grid=(N_CAST_CHUNKS_ALONE,), in_specs=[p[0] for p in plans],
        out_specs=[p[1] for p in plans], out_shape=[p[2] for p in plans],
        compiler_params=pltpu.CompilerParams(
            dimension_semantics=("arbitrary",), vmem_limit_bytes=VMEM_LIMIT_BYTES),
        name="cast_bf16")(*[w for w, _ in weights])


def _split_refs(refs, n_in, n_cast):
    bounds = [0, n_in, n_in + n_cast, n_in + n_cast + 1, n_in + 2 * n_cast + 1, len(refs)]
    ins, srcs, (o_ref,), dsts, scratch = [refs[lo:hi] for lo, hi in zip(bounds[:-1], bounds[1:])]
    return ins, srcs, o_ref, dsts, scratch


def _call(kern, name, tm, in_specs, args, next_weights, scratch_shapes):
    n_tok = args[0].shape[0]
    plans = [_cast_plan(w, layer) for w, layer in next_weights]
    outs = pl.pallas_call(
        kern,
        grid=(n_tok // tm,),
        in_specs=in_specs + [p[0] for p in plans],
        out_specs=[_tile_spec(tm)] + [p[1] for p in plans],
        out_shape=[jax.ShapeDtypeStruct((n_tok, D_MODEL), F32)] + [p[2] for p in plans],
        scratch_shapes=scratch_shapes,
        compiler_params=pltpu.CompilerParams(
            dimension_semantics=("arbitrary",), vmem_limit_bytes=VMEM_LIMIT_BYTES),
        name=name,
    )(*args, *[w for w, _ in next_weights])
    return outs[0], outs[1:]


def _make_ffn_kernel(final, n_cast):
    n_in = 5 if final else 4

    def kern(*refs):
        ins, srcs, o_ref, dsts, (a_ref,) = _split_refs(refs, n_in, n_cast)
        x_ref, g_ref, win_ref, wout_ref = ins[:4]
        _cast_step(pl.program_id(0), srcs, dsts)
        x = x_ref[...]
        xg, rinv = _norm_parts(x, g_ref[...])
        for c in range(N_FF_CHUNKS):
            cols = slice(c * FF_CHUNK, (c + 1) * FF_CHUNK)
            up_cols = slice(D_FF + c * FF_CHUNK, D_FF + (c + 1) * FF_CHUNK)
            gate = jnp.dot(xg, win_ref[:, cols], preferred_element_type=F32) * rinv
            up = jnp.dot(xg, win_ref[:, up_cols], preferred_element_type=F32) * rinv
            a_ref[:, cols] = (jax.nn.silu(gate) * up).astype(BF16)
        y = jnp.dot(a_ref[...], wout_ref[...], preferred_element_type=F32)
        out = x + 0.5 * y
        if final:
            out = _rms(out, ins[4][...])
        o_ref[...] = out

    return kern


def _ffn(x2, gain, win, wout, next_weights, final_gain=None):
    in_specs = [_tile_spec(TM_FFN), _const_spec((1, D_MODEL)), _const_spec(win.shape),
                _const_spec(wout.shape)]
    args = [x2, gain.reshape(1, D_MODEL), win, wout]
    final = final_gain is not None
    if final:
        in_specs.append(_const_spec((1, D_MODEL)))
        args.append(final_gain.reshape(1, D_MODEL))
    return _call(_make_ffn_kernel(final, len(next_weights)), "ffn_final" if final else "ffn",
                 TM_FFN, in_specs, args, next_weights, [pltpu.VMEM((TM_FFN, D_FF), BF16)])


def _make_pool_conv_kernel(tiles_per_seq, n_cast):
    def kern(*refs):
        ins, srcs, o_ref, dsts, scratch = _split_refs(refs, 11, n_cast)
        (x_ref, g_ref, win_ref, pw_ref, pb_ref, ps_ref, cw_ref, cb_ref,
         lg_ref, lb_ref, wout_ref) = ins
        ubuf, s2buf, s4buf, s8buf, g2, ybuf = scratch
        step = pl.program_id(0)
        t = step % tiles_per_seq
        rows = HALO + TM_MIX
        _cast_step(step, srcs, dsts)

        @pl.when(t == 0)
        def _():
            ubuf[0:HALO, :] = jnp.zeros((HALO, POOL_CH), F32)
            for lb in range(CONV_CH // LANES):
                _put_rows(g2, lb, 0, jnp.zeros((HALO, LANES), F32))

        x = x_ref[...]
        xn = _rms(x, g_ref[...]).astype(BF16)
        gate = jnp.dot(xn, win_ref[:, POOL_CH + CONV_CH:], preferred_element_type=F32)
        sig = jax.nn.sigmoid(gate)
        u = jnp.dot(xn, win_ref[:, :POOL_CH], preferred_element_type=F32)
        ubuf[HALO:, :] = u
        glu = jnp.dot(xn, win_ref[:, POOL_CH:POOL_CH + CONV_CH], preferred_element_type=F32) * sig
        for lb in range(CONV_CH // LANES):
            _put_rows(g2, lb, HALO, glu[:, lb * LANES:(lb + 1) * LANES])

        g1 = POOL_GC
        s2buf[8:, :] = ubuf[8:, :] + _rows_earlier(ubuf[...], 1)
        s4buf[16:, :] = s2buf[16:, g1:] + _rows_earlier(s2buf[8:, g1:], 2)
        s8buf[24:, :] = s4buf[24:, g1:] + _rows_earlier(s4buf[16:, g1:], 4)
        s16 = s8buf[HALO:, g1:] + s8buf[pl.ds(HALO - 8, TM_MIX), g1:]
        sums = [s2buf[HALO:, :g1], s4buf[HALO:, :g1], s8buf[HALO:, :g1], s16]
        pos = t * TM_MIX + lax.broadcasted_iota(jnp.int32, (TM_MIX, 1), 0)
        pooled = []
        for gi, win in enumerate(POOL_WINDOWS):
            count = jnp.minimum(pos + 1, win).astype(F32)
            pooled.append(sums[gi] / count - u[:, gi * POOL_GC:(gi + 1) * POOL_GC])
        ya = []
        for p in range(2):
            pp = jnp.concatenate(pooled[2 * p:2 * p + 2], axis=-1).astype(BF16)
            ya.append(jnp.dot(pp, pw_ref[p], preferred_element_type=F32))
        ya = (jnp.concatenate(ya, axis=-1) + pb_ref[...]) * ps_ref[...]

        taps_by_shift = [[] for _ in range(8)]
        for k in range(CONV_WIDTH):
            q, r = divmod(HALO - (CONV_WIDTH - 1) + k, 8)
            taps_by_shift[r].append((k, q))

        def conv_rows(rb, carry):
            base = pl.multiple_of(rb * CONV_RB, CONV_RB)
            for lb in range(CONV_CH // LANES):
                lanes = slice(lb * LANES, (lb + 1) * LANES)
                acc = None
                for r, taps in enumerate(taps_by_shift):
                    span = CONV_RB + 8 * max(q for _, q in taps)
                    slab = _get_rows(g2, lb, base + r, span)
                    for k, q in taps:
                        term = slab[8 * q:8 * q + CONV_RB] * cw_ref[k:k + 1, lanes]
                        acc = term if acc is None else acc + term
                ybuf[pl.ds(base, CONV_RB), lanes] = acc
            return carry

        lax.fori_loop(0, TM_MIX // CONV_RB, conv_rows, 0, unroll=4)
        y = ybuf[...] + cb_ref[...]
        yb = jax.nn.silu(_layer_norm(y, lg_ref[...], lb_ref[...]))

        cat = jnp.concatenate([ya.astype(BF16), yb.astype(BF16)], axis=-1)
        o_ref[...] = x + jnp.dot(cat, wout_ref[...], preferred_element_type=F32)

        ubuf[0:HALO, :] = ubuf[TM_MIX:, :]
        for lb in range(CONV_CH // LANES):
            _put_rows(g2, lb, 0, _get_rows(g2, lb, TM_MIX, HALO))

    return kern


def _pool_conv_mixer(x2, t_len, gain, win, wout, pool_w, pool_b, pool_scale, conv_w, conv_b,
                     ln_g, ln_b, next_weights):
    rows = HALO + TM_MIX
    pw = pool_w.astype(BF16)
    z = jnp.zeros((POOL_GC, POOL_GC), BF16)
    pw_bd = jnp.stack([
        jnp.block([[pw[0], z], [z, pw[1]]]),
        jnp.block([[pw[2], z], [z, pw[3]]]),
    ])
    row = lambda n: _const_spec((1, n))
    in_specs = [_tile_spec(TM_MIX), row(D_MODEL), _const_spec(win.shape), _const_spec(pw_bd.shape),
                row(POOL_CH), row(POOL_CH), _const_spec((CONV_WIDTH, CONV_CH)), row(CONV_CH),
                row(CONV_CH), row(CONV_CH), _const_spec(wout.shape)]
    args = [x2, gain.reshape(1, D_MODEL), win, pw_bd, pool_b.reshape(1, POOL_CH),
            pool_scale.reshape(1, POOL_CH), conv_w, conv_b.reshape(1, CONV_CH),
            ln_g.reshape(1, CONV_CH), ln_b.reshape(1, CONV_CH), wout]
    scratch = [pltpu.VMEM((rows, POOL_CH), F32),
               pltpu.VMEM((rows, POOL_CH), F32),
               pltpu.VMEM((rows, POOL_CH - POOL_GC), F32),
               pltpu.VMEM((rows, POOL_CH - 2 * POOL_GC), F32),
               pltpu.VMEM((CONV_CH // LANES, 2 * rows, LANES), F32),
               pltpu.VMEM((TM_MIX, CONV_CH), F32)]
    return _call(_make_pool_conv_kernel(t_len // TM_MIX, len(next_weights)), "pool_conv_mixer",
                 TM_MIX, in_specs, args, next_weights, scratch)


def _make_sgu_kernel(n_cast):
    def kern(*refs):
        ins, srcs, o_ref, dsts, (u_ref, v_ref, gated_ref, wm_ref) = _split_refs(refs, 8, n_cast)
        x_ref, g_ref, win_ref, lg_ref, lb_ref, ws_ref, bs_ref, wout_ref = ins
        n_chunks = TM_SGU // CHUNK
        n_cols = SGU_CH // MXU_TILE
        _cast_step(pl.program_id(0), srcs, dsts)

        @pl.when(pl.program_id(0) == 0)
        def _():
            row = lax.broadcasted_iota(jnp.int32, (CHUNK, CHUNK), 0)
            col = lax.broadcasted_iota(jnp.int32, (CHUNK, CHUNK), 1)
            mask = (col <= row).astype(F32)
            for hd in range(SGU_HEADS):
                wm_ref[hd] = (ws_ref[hd] * mask).astype(BF16)

        x = x_ref[...]
        xn = _rms(x, g_ref[...]).astype(BF16)
        zv = []
        for c in range(n_cols):
            w_cols = slice(SGU_CH + c * MXU_TILE, SGU_CH + (c + 1) * MXU_TILE)
            zv.append(_gelu(jnp.dot(xn, win_ref[:, w_cols], preferred_element_type=F32)))
        heads_per_tile = MXU_TILE // SGU_HC
        for c in range(n_cols):
            cols = slice(c * MXU_TILE, (c + 1) * MXU_TILE)
            uc = _gelu(jnp.dot(xn, win_ref[:, cols], preferred_element_type=F32))
            for k in range(heads_per_tile):
                u_ref[c * heads_per_tile + k] = uc[:, k * SGU_HC:(k + 1) * SGU_HC]
        v = _layer_norm(jnp.concatenate(zv, axis=-1), lg_ref[...], lb_ref[...]).astype(BF16)
        for hd in range(SGU_HEADS):
            v_ref[hd] = v[:, hd * SGU_HC:(hd + 1) * SGU_HC]

        for hd in range(SGU_HEADS):
            lanes = slice(hd * SGU_HC, (hd + 1) * SGU_HC)
            w = wm_ref[hd]
            rhs = jnp.concatenate(
                [v_ref[hd, n * CHUNK:(n + 1) * CHUNK, :] for n in range(n_chunks)], axis=-1)
            r = jnp.dot(w, rhs, preferred_element_type=F32)
            for n in range(n_chunks):
                rows_n = slice(n * CHUNK, (n + 1) * CHUNK)
                vo = r[:, n * SGU_HC:(n + 1) * SGU_HC] + bs_ref[:, lanes]
                gated_ref[rows_n, lanes] = (u_ref[hd, rows_n, :] * vo).astype(BF16)
        o_ref[...] = x + jnp.dot(gated_ref[...], wout_ref[...], preferred_element_type=F32)

    return kern


def _sgu_mixer(x2, gain, win, wout, ln_g, ln_b, w_s, b_s, next_weights):
    bias = jnp.repeat(b_s.T, SGU_HC, axis=1)
    in_specs = [_tile_spec(TM_SGU), _const_spec((1, D_MODEL)), _const_spec(win.shape),
                _const_spec((1, SGU_CH)), _const_spec((1, SGU_CH)),
                _const_spec((SGU_HEADS, CHUNK, CHUNK)), _const_spec((CHUNK, SGU_CH)),
                _const_spec(wout.shape)]
    args = [x2, gain.reshape(1, D_MODEL), win, ln_g.reshape(1, SGU_CH), ln_b.reshape(1, SGU_CH),
            w_s, bias, wout]
    scratch = [pltpu.VMEM((SGU_HEADS, TM_SGU, SGU_HC), F32),
               pltpu.VMEM((SGU_HEADS, TM_SGU, SGU_HC), BF16),
               pltpu.VMEM((TM_SGU, SGU_CH), BF16),
               pltpu.VMEM((SGU_HEADS, CHUNK, CHUNK), BF16)]
    return _call(_make_sgu_kernel(len(next_weights)), "sgu_mixer", TM_SGU, in_specs, args,
                 next_weights, scratch)


def kernel(x, ffn1_norm, ffn1_w_in, ffn1_w_out, mix_norm, ffn2_norm, ffn2_w_in, ffn2_w_out,
           ab_w_in, pool_w, pool_b, pool_scale, conv_w, conv_b, conv_ln_g, conv_ln_b, ab_w_out,
           sgu_w_in, sgu_ln_g, sgu_ln_b, sgu_w, sgu_b, sgu_w_out, final_norm):
    bsz, t_len, d = x.shape
    depth = ffn1_norm.shape[0]
    assert d == D_MODEL and t_len % max(TM_MIX, TM_SGU, TM_FFN) == 0
    assert (bsz * t_len) // max(TM_MIX, TM_SGU, TM_FFN) >= N_CAST_CHUNKS
    x2 = x.reshape(bsz * t_len, d)
    w = _cast_bf16([(ffn1_w_in, 0), (ffn1_w_out, 0)])
    for i in range(depth):
        j = i // 2
        pool_layer = i % 2 == 0
        mixer_weights = [(ab_w_in, j), (ab_w_out, j)] if pool_layer else [(sgu_w_in, j), (sgu_w_out, j)]
        x2, w = _ffn(x2, ffn1_norm[i], w[0], w[1], mixer_weights + [(ffn2_w_in, i), (ffn2_w_out, i)])
        w_mix, w_ffn2 = w[:2], w[2:]
        if pool_layer:
            x2, _ = _pool_conv_mixer(x2, t_len, mix_norm[i], w_mix[0], w_mix[1], pool_w[j], pool_b[j],
                                     pool_scale[j], conv_w[j], conv_b[j], conv_ln_g[j], conv_ln_b[j], [])
        else:
            x2, _ = _sgu_mixer(x2, mix_norm[i], w_mix[0], w_mix[1], sgu_ln_g[j], sgu_ln_b[j], sgu_w[j],
                               sgu_b[j], [])
        last = i == depth - 1
        next_ffn1 = [] if last else [(ffn1_w_in, i + 1), (ffn1_w_out, i + 1)]
        x2, w = _ffn(x2, ffn2_norm[i], w_ffn2[0], w_ffn2[1], next_ffn1, final_norm if last else None)
    return x2.reshape(bsz, t_len, d)
```
